```python
import math, functools
import jax, jax.numpy as jnp
from jax import lax
import numpy as np

D_MODEL = 2048
BATCH = 8
SEQ = 2048
DEPTH = 2
DEC_BATCH = 128
DEC_SEQ = 4
PAST_LEN = 2048
PAGE_SIZE = 128

ATT_HD = 128
ATT_W = D_MODEL // 2
ATT_HEADS = ATT_W // ATT_HD
IDX_HEADS = 16
IDX_HD = 64
TOPK_MAX = 256
QBLK = 64
ROPE_THETA = 10000.0
RW_HD = 64
RW_W = D_MODEL // 4
RW_HEADS = RW_W // RW_HD
RW_DECAY_R = 32
RW_A_R = 32
RW_G_R = 64
RW_IN = 3 * RW_W + RW_DECAY_R + RW_A_R + RW_G_R
S5_W = D_MODEL - ATT_W - RW_W
S5_GC = 16
S5_GROUPS = S5_W // S5_GC
S5_P = 64
MIX_W = ATT_W + RW_W + S5_W
ATT_IN = 3 * ATT_W + IDX_HEADS * IDX_HD + IDX_HD + IDX_HEADS
N_IN = ATT_IN + RW_IN + S5_W
D_FF = ((8 * D_MODEL // 3 + 63) // 64) * 64
NORM_EPS = 1e-6
RW_LN_EPS = 64e-5

kernel_name = 'hymba_dsa_rwkv7_s5_macaron_step'


def _split(z, sizes):
    offs, s = [], 0
    for n in sizes[:-1]:
        s += n
        offs.append(s)
    return jnp.split(z, offs, axis=-1)


def _rmsnorm(x, g):
    xf = x.astype(jnp.float32)
    y = xf * lax.rsqrt(jnp.mean(xf * xf, axis=-1, keepdims=True) + NORM_EPS)
    return (y * g.astype(jnp.float32)).astype(x.dtype)


def _rope(x, pos):
    half = x.shape[-1] // 2
    inv = ROPE_THETA ** (-jnp.arange(half, dtype=jnp.float32) / half)
    ang = pos.astype(jnp.float32)[:, None] * inv[None, :]
    cos = jnp.cos(ang)[None, :, None, :]
    sin = jnp.sin(ang)[None, :, None, :]
    xf = x.astype(jnp.float32)
    x1, x2 = xf[..., :half], xf[..., half:]
    return jnp.concatenate([x1 * cos - x2 * sin, x2 * cos + x1 * sin], axis=-1).astype(x.dtype)


def _swiglu(x, wg, wu, wd):
    return (jax.nn.silu(x @ wg) * (x @ wu)) @ wd


def _attn_prep(za, pos, q_norm, k_norm):
    B, T, _ = za.shape
    q, k, v, qi, ki, wi = _split(za, (ATT_W, ATT_W, ATT_W, IDX_HEADS * IDX_HD, IDX_HD, IDX_HEADS))
    q = _rope(_rmsnorm(q.reshape(B, T, ATT_HEADS, ATT_HD), q_norm), pos)
    k = _rope(_rmsnorm(k.reshape(B, T, ATT_HEADS, ATT_HD), k_norm), pos)
    v = v.reshape(B, T, ATT_HEADS, ATT_HD)
    qi = _rope(qi.reshape(B, T, IDX_HEADS, IDX_HD), pos)
    ki = _rope(ki[:, :, None, :], pos)[:, :, 0]
    wi = wi * (IDX_HEADS * IDX_HD) ** -0.5
    return q, k, v, qi, ki, wi


def _index_scores(qi, ki, wi):
    s = jnp.einsum('bqhd,bsd->bqhs', qi, ki).astype(jnp.float32)
    return jnp.einsum('bqh,bqhs->bqs', wi.astype(jnp.float32), jax.nn.relu(s))


def _dsa_prompt(q, k, v, qi, ki, wi):
    B, T = q.shape[:2]
    topk = min(TOPK_MAX, T // 4)
    nblk = T // QBLK
    key_pos = jnp.arange(T)
    take = jax.vmap(lambda src, idx: src[idx])

    def block(i):
        t0 = i * QBLK
        qb = lax.dynamic_slice_in_dim(q, t0, QBLK, axis=1)
        qib = lax.dynamic_slice_in_dim(qi, t0, QBLK, axis=1)
        wib = lax.dynamic_slice_in_dim(wi, t0, QBLK, axis=1)
        qpos = t0 + jnp.arange(QBLK)
        causal = key_pos[None, :] <= qpos[:, None]
        scores_idx = jnp.where(causal[None], _index_scores(qib, ki, wib), -jnp.inf)
        _, sel = lax.top_k(scores_idx, topk)
        valid = sel <= qpos[None, :, None]
        ks = take(k, sel)
        vs = take(v, sel)
        s = jnp.einsum('bqhd,bqkhd->bhqk', qb, ks).astype(jnp.float32) * ATT_HD ** -0.5
        s = jnp.where(valid[:, None], s, -jnp.inf)
        p = jax.nn.softmax(s, axis=-1).astype(vs.dtype)
        return jnp.einsum('bhqk,bqkhd->bqhd', p, vs).reshape(B, QBLK, ATT_W)

    out = lax.map(block, jnp.arange(nblk))
    return jnp.moveaxis(out, 0, 1).reshape(B, T, ATT_W)


def _dsa_sample(q, k, v, qi, ki, wi, ck, cv, cki, page_table):
    DB, S = q.shape[:2]
    past = page_table.shape[1] * PAGE_SIZE
    L = past + S
    topk = min(TOPK_MAX, L // 4)
    past_ki = cki[page_table].reshape(DB, past, IDX_HD)
    ki_all = jnp.concatenate([past_ki.astype(ki.dtype), ki], axis=1)
    qpos = past + jnp.arange(S)
    causal = jnp.arange(L)[None, :] <= qpos[:, None]
    scores_idx = jnp.where(causal[None], _index_scores(qi, ki_all, wi), -jnp.inf)
    _, sel = lax.top_k(scores_idx, topk)
    in_past = sel < past
    ps = jnp.minimum(sel, past - 1)
    phys = jnp.take_along_axis(page_table, (ps // PAGE_SIZE).reshape(DB, -1), axis=1).reshape(DB, S, topk)
    slot = ps % PAGE_SIZE
    k_sel = ck[phys, slot]
    v_sel = cv[phys, slot]
    new_pos = past + jnp.arange(S)
    new_sel = jnp.any(sel[..., None] == new_pos, axis=2)
    new_ok = new_sel & (jnp.arange(S)[None, :] <= jnp.arange(S)[:, None])[None]
    s_past = jnp.einsum('bqhd,bqkhd->bhqk', q, k_sel.astype(q.dtype))
    s_new = jnp.einsum('bqhd,bjhd->bhqj', q, k)
    s = jnp.concatenate([s_past, s_new], axis=-1).astype(jnp.float32) * ATT_HD ** -0.5
    mask = jnp.concatenate([in_past, new_ok], axis=-1)
    s = jnp.where(mask[:, None], s, -jnp.inf)
    p = jax.nn.softmax(s, axis=-1).astype(v.dtype)
    out = (jnp.einsum('bhqk,bqkhd->bqhd', p[..., :topk], v_sel.astype(v.dtype))
           + jnp.einsum('bhqj,bjhd->bqhd', p[..., topk:], v))
    return out.reshape(DB, S, ATT_W)


def _rwkv_scan(state0, r, w, k, v, kk, a):
    def step(S, inp):
        r_t, w_t, k_t, v_t, kk_t, a_t = inp
        sa = jnp.einsum('bhvk,bhk->bhv', S, -kk_t)
        S = (S * w_t[:, :, None, :] + sa[..., None] * (kk_t * a_t)[:, :, None, :]
             + v_t[..., None] * k_t[:, :, None, :])
        return S, jnp.einsum('bhvk,bhk->bhv', S, r_t)
    xs = tuple(jnp.moveaxis(t.astype(jnp.float32), 1, 0) for t in (r, w, k, v, kk, a))
    S, ys = lax.scan(step, state0.astype(jnp.float32), xs)
    return S, jnp.moveaxis(ys, 0, 1)


def _rwkv(zr, shift0, state0, mu, w0, w2, a0, a2, g2, kk_gain, ka, rk, lnx_g, lnx_b):
    B, T, _ = zr.shape
    zf = zr.astype(jnp.float32)
    prev = jnp.concatenate([shift0.astype(jnp.float32)[:, None], zf[:, :-1]], axis=1)
    xm = zf + (prev - zf) * mu
    r, k, v, wd, ad, gd = _split(xm, (RW_W, RW_W, RW_W, RW_DECAY_R, RW_A_R, RW_G_R))
    decay = jnp.exp(-jnp.exp(-jax.nn.softplus(-(w0 + jnp.tanh(wd) @ w2)) - 0.5))
    a = jax.nn.sigmoid(a0 + ad @ a2)
    g = jax.nn.sigmoid(gd) @ g2
    heads = lambda t: t.reshape(B, T, RW_HEADS, RW_HD)
    kk = heads(k * kk_gain)
    kk = kk / jnp.maximum(jnp.sqrt(jnp.sum(kk * kk, axis=-1, keepdims=True)), 1e-12)
    k = k * (1.0 + (a - 1.0) * ka)
    r_h, k_h, v_h = heads(r), heads(k), heads(v)
    S, y = _rwkv_scan(state0, r_h, heads(decay), k_h, v_h, kk, heads(a))
    mean = jnp.mean(y, axis=-1, keepdims=True)
    var = jnp.mean(jnp.square(y - mean), axis=-1, keepdims=True)
    y = ((y - mean) * lax.rsqrt(var + RW_LN_EPS)).reshape(B, T, RW_W) * lnx_g + lnx_b
    bonus = jnp.sum(r_h * k_h * rk, axis=-1, keepdims=True) * v_h
    out = (y + bonus.reshape(B, T, RW_W)) * g
    return out.astype(zr.dtype), zr[:, -1], S


def _cplx_combine(left, right):
    a_re_l, a_im_l, b_re_l, b_im_l = left
    a_re_r, a_im_r, b_re_r, b_im_r = right
    return (a_re_r * a_re_l - a_im_r * a_im_l,
            a_re_r * a_im_l + a_im_r * a_re_l,
            a_re_r * b_re_l - a_im_r * b_im_l + b_re_r,
            a_re_r * b_im_l + a_im_r * b_re_l + b_im_r)


def _s5(u, x0_re, x0_im, lam_re, lam_im, log_step, b_re, b_im, c_re, c_im, dskip, glu1, glu2):
    B, T, _ = u.shape
    f32 = jnp.float32
    delta = jnp.exp(log_step.astype(f32))[:, None]
    lr, li = lam_re.astype(f32), lam_im.astype(f32)
    mag = jnp.exp(lr * delta)
    ab_re, ab_im = mag * jnp.cos(li * delta), mag * jnp.sin(li * delta)
    den = lr * lr + li * li
    n_re, n_im = ab_re - 1.0, ab_im
    q_re = (n_re * lr + n_im * li) / den
    q_im = (n_im * lr - n_re * li) / den
    bb_re = q_re[..., None] * b_re - q_im[..., None] * b_im
    bb_im = q_re[..., None] * b_im + q_im[..., None] * b_re
    uf = u.astype(f32)
    ug = uf.reshape(B, T, S5_GROUPS, S5_GC)
    bu_re = jnp.einsum('btgc,gpc->btgp', ug, bb_re)
    bu_im = jnp.einsum('btgc,gpc->btgp', ug, bb_im)
    x0r, x0i = x0_re.astype(f32), x0_im.astype(f32)
    bu_re = bu_re.at[:, 0].add(ab_re * x0r - ab_im * x0i)
    bu_im = bu_im.at[:, 0].add(ab_re * x0i + ab_im * x0r)
    a_re = jnp.broadcast_to(ab_re, bu_re.shape)
    a_im = jnp.broadcast_to(ab_im, bu_im.shape)
    _, _, xs_re, xs_im = lax.associative_scan(_cplx_combine, (a_re, a_im, bu_re, bu_im), axis=1)
    y = jnp.einsum('btgp,gcp->btgc', xs_re, c_re) - jnp.einsum('btgp,gcp->btgc', xs_im, c_im)
    y = y.reshape(B, T, S5_W) + dskip * uf
    y = jax.nn.gelu(y)
    out = (y @ glu1) * jax.nn.sigmoid(y @ glu2)
    return out.astype(u.dtype), xs_re[:, -1], xs_im[:, -1]


def _layer(x, pos, l, p, attn_fn, shift0, rw0, s5re0, s5im0):
    x = x + 0.5 * _swiglu(_rmsnorm(x, p['ffn1_norm'][l]), p['ffn1_wg'][l], p['ffn1_wu'][l], p['ffn1_wd'][l])
    z = _rmsnorm(x, p['mix_norm'][l]) @ p['w_in'][l]
    za, zr, zs = _split(z, (ATT_IN, RW_IN, S5_W))
    q, k, v, qi, ki, wi = _attn_prep(za, pos, p['q_norm'][l], p['k_norm'][l])
    ya = attn_fn(q, k, v, qi, ki, wi)
    yr, shift_new, rw_new = _rwkv(zr, shift0, rw0, p['rw_mu'][l], p['rw_w0'][l], p['rw_w2'][l],
                                  p['rw_a0'][l], p['rw_a2'][l], p['rw_g2'][l], p['rw_kk'][l],
                                  p['rw_ka'][l], p['rw_rk'][l], p['rw_lnx_g'][l], p['rw_lnx_b'][l])
    ys, s5re, s5im = _s5(zs, s5re0, s5im0, p['s5_lam_re'][l], p['s5_lam_im'][l], p['s5_log_step'][l],
                         p['s5_b_re'][l], p['s5_b_im'][l], p['s5_c_re'][l], p['s5_c_im'][l],
                         p['s5_d'][l], p['s5_glu1'][l], p['s5_glu2'][l])
    y_mix = jnp.concatenate([ya.astype(x.dtype), yr.astype(x.dtype), ys.astype(x.dtype)], axis=-1)
    x = x + y_mix @ p['w_out'][l]
    x = x + 0.5 * _swiglu(_rmsnorm(x, p['ffn2_norm'][l]), p['ffn2_wg'][l], p['ffn2_wu'][l], p['ffn2_wd'][l])
    return x, (k, v, ki, rw_new, shift_new, s5re, s5im)


def setup_inputs(seed: int = 0) -> dict:
    key = jax.random.key(seed)
    ks = iter(jax.random.split(key, 64))
    f32 = jnp.float32
    nrm = lambda shape, scale: scale * jax.random.normal(next(ks), shape, f32)
    uni = lambda shape, lo, hi: jax.random.uniform(next(ks), shape, f32, lo, hi)
    n_pages = PAST_LEN // PAGE_SIZE
    n_used = DEC_BATCH * n_pages
    n_pool = n_used + max(1, n_used // 4)
    inp = {}
    inp['x_prompt'] = nrm((BATCH, SEQ, D_MODEL), 1.0)
    inp['x_sample'] = nrm((DEC_BATCH, DEC_SEQ, D_MODEL), 1.0)
    inp['cache_k'] = nrm((DEPTH, n_pool, PAGE_SIZE, ATT_HEADS, ATT_HD), 1.0)
    inp['cache_v'] = nrm((DEPTH, n_pool, PAGE_SIZE, ATT_HEADS, ATT_HD), 1.0)
    inp['cache_kidx'] = nrm((DEPTH, n_pool, PAGE_SIZE, IDX_HD), 1.0)
    inp['state_rwkv'] = nrm((DEPTH, DEC_BATCH, RW_HEADS, RW_HD, RW_HD), 0.3)
    inp['state_shift'] = nrm((DEPTH, DEC_BATCH, RW_IN), 1.0)
    inp['state_s5_re'] = nrm((DEPTH, DEC_BATCH, S5_GROUPS, S5_P), 0.5)
    inp['state_s5_im'] = nrm((DEPTH, DEC_BATCH, S5_GROUPS, S5_P), 0.5)
    inp['page_table'] = jax.random.permutation(next(ks), n_pool)[:n_used].reshape(DEC_BATCH, n_pages).astype(jnp.int32)
    inp['ffn1_norm'] = 1.0 + nrm((DEPTH, D_MODEL), 0.02)
    inp['ffn1_wg'] = nrm((DEPTH, D_MODEL, D_FF), D_MODEL ** -0.5)
    inp['ffn1_wu'] = nrm((DEPTH, D_MODEL, D_FF), D_MODEL ** -0.5)
    inp['ffn1_wd'] = nrm((DEPTH, D_FF, D_MODEL), D_FF ** -0.5)
    inp['mix_norm'] = 1.0 + nrm((DEPTH, D_MODEL), 0.02)
    inp['w_in'] = nrm((DEPTH, D_MODEL, N_IN), D_MODEL ** -0.5)
    inp['q_norm'] = 1.0 + nrm((DEPTH, ATT_HD), 0.02)
    inp['k_norm'] = 1.0 + nrm((DEPTH, ATT_HD), 0.02)
    inp['rw_mu'] = uni((DEPTH, RW_IN), 0.0, 1.0)
    inp['rw_w0'] = uni((DEPTH, RW_W), -4.0, 0.0)
    inp['rw_w2'] = nrm((DEPTH, RW_DECAY_R, RW_W), 0.5 * RW_DECAY_R ** -0.5)
    inp['rw_a0'] = nrm((DEPTH, RW_W), 0.1)
    inp['rw_a2'] = nrm((DEPTH, RW_A_R, RW_W), 0.5 * RW_A_R ** -0.5)
    inp['rw_g2'] = nrm((DEPTH, RW_G_R, RW_W), RW_G_R ** -0.5)
    inp['rw_kk'] = 0.85 + nrm((DEPTH, RW_W), 0.02)
    inp['rw_ka'] = 1.0 + nrm((DEPTH, RW_W), 0.02)
    inp['rw_rk'] = nrm((DEPTH, RW_HEADS, RW_HD), 0.1)
    inp['rw_lnx_g'] = 1.0 + nrm((DEPTH, RW_W), 0.02)
    inp['rw_lnx_b'] = nrm((DEPTH, RW_W), 0.02)
    inp['s5_lam_re'] = -0.5 + nrm((DEPTH, S5_GROUPS, S5_P), 0.01)
    inp['s5_lam_im'] = jnp.pi * jnp.arange(S5_P, dtype=f32) + nrm((DEPTH, S5_GROUPS, S5_P), 0.01)
    inp['s5_log_step'] = uni((DEPTH, S5_GROUPS), math.log(1e-3), math.log(1e-1))
    inp['s5_b_re'] = nrm((DEPTH, S5_GROUPS, S5_P, S5_GC), (2 * S5_GC) ** -0.5)
    inp['s5_b_im'] = nrm((DEPTH, S5_GROUPS, S5_P, S5_GC), (2 * S5_GC) ** -0.5)
    inp['s5_c_re'] = nrm((DEPTH, S5_GROUPS, S5_GC, S5_P), S5_P ** -0.5)
    inp['s5_c_im'] = nrm((DEPTH, S5_GROUPS, S5_GC, S5_P), S5_P ** -0.5)
    inp['s5_d'] = nrm((DEPTH, S5_W), 0.5)
    inp['s5_glu1'] = nrm((DEPTH, S5_W, S5_W), S5_W ** -0.5)
    inp['s5_glu2'] = nrm((DEPTH, S5_W, S5_W), S5_W ** -0.5)
    inp['w_out'] = nrm((DEPTH, MIX_W, D_MODEL), MIX_W ** -0.5)
    inp['ffn2_norm'] = 1.0 + nrm((DEPTH, D_MODEL), 0.02)
    inp['ffn2_wg'] = nrm((DEPTH, D_MODEL, D_FF), D_MODEL ** -0.5)
    inp['ffn2_wu'] = nrm((DEPTH, D_MODEL, D_FF), D_MODEL ** -0.5)
    inp['ffn2_wd'] = nrm((DEPTH, D_FF, D_MODEL), D_FF ** -0.5)
    return inp


def reference(x_prompt, x_sample, cache_k, cache_v, cache_kidx, state_rwkv, state_shift,
              state_s5_re, state_s5_im, page_table,
              ffn1_norm, ffn1_wg, ffn1_wu, ffn1_wd, mix_norm, w_in, q_norm, k_norm,
              rw_mu, rw_w0, rw_w2, rw_a0, rw_a2, rw_g2, rw_kk, rw_ka, rw_rk, rw_lnx_g, rw_lnx_b,
              s5_lam_re, s5_lam_im, s5_log_step, s5_b_re, s5_b_im, s5_c_re, s5_c_im, s5_d,
              s5_glu1, s5_glu2, w_out, ffn2_norm, ffn2_wg, ffn2_wu, ffn2_wd):
    p = dict(ffn1_norm=ffn1_norm, ffn1_wg=ffn1_wg, ffn1_wu=ffn1_wu, ffn1_wd=ffn1_wd,
             mix_norm=mix_norm, w_in=w_in, q_norm=q_norm, k_norm=k_norm,
             rw_mu=rw_mu, rw_w0=rw_w0, rw_w2=rw_w2, rw_a0=rw_a0, rw_a2=rw_a2, rw_g2=rw_g2,
             rw_kk=rw_kk, rw_ka=rw_ka, rw_rk=rw_rk, rw_lnx_g=rw_lnx_g, rw_lnx_b=rw_lnx_b,
             s5_lam_re=s5_lam_re, s5_lam_im=s5_lam_im, s5_log_step=s5_log_step,
             s5_b_re=s5_b_re, s5_b_im=s5_b_im, s5_c_re=s5_c_re, s5_c_im=s5_c_im, s5_d=s5_d,
             s5_glu1=s5_glu1, s5_glu2=s5_glu2, w_out=w_out,
             ffn2_norm=ffn2_norm, ffn2_wg=ffn2_wg, ffn2_wu=ffn2_wu, ffn2_wd=ffn2_wd)
    f32 = jnp.float32
    B, T = x_prompt.shape[:2]
    pos_p = jnp.arange(T)
    y_prompt = x_prompt
    st_p = []
    for l in range(DEPTH):
        y_prompt, st = _layer(y_prompt, pos_p, l, p, _dsa_prompt,
                              jnp.zeros((B, RW_IN), f32),
                              jnp.zeros((B, RW_HEADS, RW_HD, RW_HD), f32),
                              jnp.zeros((B, S5_GROUPS, S5_P), f32),
                              jnp.zeros((B, S5_GROUPS, S5_P), f32))
        st_p.append(st)
    k_prompt, v_prompt, kidx_prompt, rwkv_prompt, shift_prompt, s5re_prompt, s5im_prompt = [
        jnp.stack(t) for t in zip(*st_p)]
    past = page_table.shape[1] * PAGE_SIZE
    pos_s = past + jnp.arange(x_sample.shape[1])
    y_sample = x_sample
    st_s = []
    for l in range(DEPTH):
        attn = functools.partial(_dsa_sample, ck=cache_k[l], cv=cache_v[l], cki=cache_kidx[l],
                                 page_table=page_table)
        y_sample, st = _layer(y_sample, pos_s, l, p, attn, state_shift[l], state_rwkv[l],
                              state_s5_re[l], state_s5_im[l])
        st_s.append(st)
    k_sample, v_sample, kidx_sample, rwkv_sample, shift_sample, s5re_sample, s5im_sample = [
        jnp.stack(t) for t in zip(*st_s)]
    return (y_prompt, y_sample,
            k_prompt, v_prompt, kidx_prompt, rwkv_prompt, shift_prompt, s5re_prompt, s5im_prompt,
            k_sample, v_sample, kidx_sample, rwkv_sample, shift_sample, s5re_sample, s5im_sample)
```

```python
import functools
import math

import jax
import jax.numpy as jnp
from jax import lax
from jax.experimental import pallas as pl
from jax.experimental.pallas import tpu as pltpu

F32 = jnp.float32
BF16 = jnp.bfloat16
HI = lax.Precision.HIGHEST

D_MODEL = 2048
PAGE_SIZE = 128
ATT_HD = 128
ATT_W = 1024
ATT_HEADS = 8
IDX_HEADS = 16
IDX_HD = 64
IDX_W = IDX_HEADS * IDX_HD
IDX_WX = IDX_HEADS * 128
TOPK_MAX = 256
ROPE_THETA = 10000.0
RW_HD = 64
RW_W = 512
RW_HEADS = 8
RW_DECAY_R = 32
RW_A_R = 32
RW_G_R = 64
RW_IN = 3 * RW_W + RW_DECAY_R + RW_A_R + RW_G_R
S5_W = 512
S5_GC = 16
S5_GROUPS = 32
S5_P = 64
ATT_IN = 3 * ATT_W + IDX_W + IDX_HD + IDX_HEADS
D_FF = 5504
NORM_EPS = 1e-6
RW_LN_EPS = 64e-5

LANES = 128
VMEM_LIMIT = 52 * 1024 * 1024
FF_TILE = 512
D_FF_PAD = ((D_FF + FF_TILE - 1) // FF_TILE) * FF_TILE
ATT_Z = 4 * ATT_W + LANES
INT_MIN = -2147483648
NEG_INF = float("-inf")


def _cparams(*sem):
    return pltpu.CompilerParams(dimension_semantics=sem, vmem_limit_bytes=VMEM_LIMIT)


def _nt(a, b, precision=None):
    return lax.dot_general(a, b, (((1,), (1,)), ((), ())), precision=precision,
                           preferred_element_type=F32)


def _tn(a, b, precision=None):
    return lax.dot_general(a, b, (((0,), (0,)), ((), ())), precision=precision,
                           preferred_element_type=F32)


def _sigmoid(x):
    return 1.0 / (1.0 + jnp.exp(-x))


def _rms(x, g):
    ms = jnp.mean(x * x, axis=-1, keepdims=True)
    return x * lax.rsqrt(ms + NORM_EPS) * g


def _ffn_kernel(x_ref, g_ref, wg_ref, wu_ref, wd_ref, o_ref, xn_ref):
    j = pl.program_id(1)

    @pl.when(j == 0)
    def _():
        xn_ref[...] = _rms(x_ref[...], g_ref[...]).astype(BF16)
        o_ref[...] = jnp.zeros_like(o_ref)

    xn = xn_ref[...]
    hg = jnp.dot(xn, wg_ref[...], preferred_element_type=F32)
    hu = jnp.dot(xn, wu_ref[...], preferred_element_type=F32)
    h = (hg * _sigmoid(hg) * hu).astype(BF16)
    o_ref[...] += jnp.dot(h, wd_ref[...], preferred_element_type=F32)

    @pl.when(j == pl.num_programs(1) - 1)
    def _():
        o_ref[...] = x_ref[...] + 0.5 * o_ref[...]


def _ffn(x, g, wg, wu, wd):
    m, d = x.shape
    tm = min(512, m)
    nf = wg.shape[1] // FF_TILE
    return pl.pallas_call(
        _ffn_kernel,
        grid=(m // tm, nf),
        in_specs=[
            pl.BlockSpec((tm, d), lambda i, j: (i, 0)),
            pl.BlockSpec((1, d), lambda i, j: (0, 0)),
            pl.BlockSpec((d, FF_TILE), lambda i, j: (0, j)),
            pl.BlockSpec((d, FF_TILE), lambda i, j: (0, j)),
            pl.BlockSpec((FF_TILE, d), lambda i, j: (j, 0)),
        ],
        out_specs=pl.BlockSpec((tm, d), lambda i, j: (i, 0)),
        out_shape=jax.ShapeDtypeStruct((m, d), F32),
        scratch_shapes=[pltpu.VMEM((tm, d), BF16)],
        compiler_params=_cparams("parallel", "arbitrary"),
        name="ffn",
    )(x, g, wg, wu, wd)


def _nmm_kernel(x_ref, g_ref, w_ref, o_ref, xn_ref):
    @pl.when(pl.program_id(1) == 0)
    def _():
        xn_ref[...] = _rms(x_ref[...], g_ref[...]).astype(BF16)

    o_ref[...] = jnp.dot(xn_ref[...], w_ref[...], preferred_element_type=F32)


def _nmm(x, g, w, tn):
    m, d = x.shape
    n = w.shape[1]
    tm = min(512, m)
    return pl.pallas_call(
        _nmm_kernel,
        grid=(m // tm, n // tn),
        in_specs=[
            pl.BlockSpec((tm, d), lambda i, j: (i, 0)),
            pl.BlockSpec((1, d), lambda i, j: (0, 0)),
            pl.BlockSpec((d, tn), lambda i, j: (0, j)),
        ],
        out_specs=pl.BlockSpec((tm, tn), lambda i, j: (i, j)),
        out_shape=jax.ShapeDtypeStruct((m, n), F32),
        scratch_shapes=[pltpu.VMEM((tm, d), BF16)],
        compiler_params=_cparams("parallel", "arbitrary"),
        name="norm_proj",
    )(x, g, w)


def _rope_tables(pos):
    def tab(half, reps):
        inv = ROPE_THETA ** (-jnp.arange(half, dtype=F32) / half)
        ang = pos.astype(F32)[:, None] * inv[None, :]
        c, s = jnp.cos(ang), jnp.sin(ang)
        return jnp.tile(jnp.concatenate([c, c], -1), (1, reps)), jnp.tile(jnp.concatenate([-s, s], -1), (1, reps))
    c128, s128 = tab(ATT_HD // 2, 1)
    c64, s64 = tab(IDX_HD // 2, 2)
    return c128, s128, c64, s64


def _rope128(x, c, s):
    return x * c + pltpu.roll(x, ATT_HD // 2, 1) * s


def _rope64(x, c, s):
    lane = lax.broadcasted_iota(jnp.int32, x.shape, 1)
    first = (lane % IDX_HD) < IDX_HD // 2
    partner = jnp.where(first, pltpu.roll(x, LANES - IDX_HD // 2, 1), pltpu.roll(x, IDX_HD // 2, 1))
    return x * c + partner * s


def _prep_kernel(q_ref, k_ref, v_ref, qi_ref, kw_ref, c128_ref, s128_ref, c64_ref, s64_ref, qn_ref, kn_ref,
                 qo_ref, ko_ref, kb_ref, vb_ref, qio_ref, kio_ref, kid_ref, wio_ref):
    c128, s128 = c128_ref[...], s128_ref[...]
    c64, s64 = c64_ref[...], s64_ref[...]
    for h in range(ATT_HEADS):
        sl = slice(h * ATT_HD, (h + 1) * ATT_HD)
        q = _rope128(_rms(q_ref[:, sl], qn_ref[...]), c128, s128)
        k = _rope128(_rms(k_ref[:, sl], kn_ref[...]), c128, s128)
        qo_ref[:, sl] = q.astype(BF16)
        ko_ref[:, sl] = k
        kb_ref[:, sl] = k.astype(BF16)
    vb_ref[...] = v_ref[...].astype(BF16)
    lane = lax.broadcasted_iota(jnp.int32, c64.shape, 1)
    for p in range(IDX_W // LANES):
        qi = _rope64(qi_ref[:, p * LANES:(p + 1) * LANES], c64, s64)
        qio_ref[:, 2 * p * LANES:(2 * p + 1) * LANES] = jnp.where(lane < IDX_HD, qi, 0.0).astype(BF16)
        qio_ref[:, (2 * p + 1) * LANES:(2 * p + 2) * LANES] = jnp.where(lane < IDX_HD, 0.0, qi).astype(BF16)
    kw = kw_ref[...]
    ki = _rope64(kw, c64, s64)
    kio_ref[...] = ki[:, :IDX_HD]
    kid_ref[...] = jnp.where(lane < IDX_HD, ki, pltpu.roll(ki, IDX_HD, 1)).astype(BF16)
    wio_ref[...] = kw[:, IDX_HD:IDX_HD + IDX_HEADS] * (IDX_W ** -0.5)


def _attn_prep(za, tabs, q_norm, k_norm):
    m = za.shape[0]
    tm = min(512, m)
    nt = tabs[0].shape[0] // tm
    wide = lambda c: pl.BlockSpec((tm, ATT_W), lambda i: (i, c))
    tab = pl.BlockSpec((tm, LANES), lambda i: (i % nt, 0))
    vec = pl.BlockSpec((1, ATT_HD), lambda i: (0, 0))
    row = lambda w: pl.BlockSpec((tm, w), lambda i: (i, 0))
    sds = lambda w, dt: jax.ShapeDtypeStruct((m, w), dt)
    return pl.pallas_call(
        _prep_kernel,
        grid=(m // tm,),
        in_specs=[wide(0), wide(1), wide(2), wide(3),
                  pl.BlockSpec((tm, LANES), lambda i: (i, 4 * ATT_W // LANES)),
                  tab, tab, tab, tab, vec, vec],
        out_specs=[row(ATT_W), row(ATT_W), row(ATT_W), row(ATT_W), row(IDX_WX), row(IDX_HD), row(LANES),
                   row(IDX_HEADS)],
        out_shape=[sds(ATT_W, BF16), sds(ATT_W, F32), sds(ATT_W, BF16), sds(ATT_W, BF16), sds(IDX_WX, BF16),
                   sds(IDX_HD, F32), sds(LANES, BF16), sds(IDX_HEADS, F32)],
        compiler_params=_cparams("parallel"),
        name="attn_prep",
    )(za, za, za, za, za, *tabs, q_norm, k_norm)


def _order_key(x):
    bits = lax.bitcast_convert_type(x, jnp.int32)
    return bits ^ ((bits >> 31) & jnp.int32(0x7FFFFFFF))


def _kth_largest(count_ge, rows, k):
    def body(i, cur):
        cand = cur + (jnp.int32(1) << (31 - i))
        return jnp.where(count_ge(cand) >= k, cand, cur)
    return lax.fori_loop(0, 32, body, jnp.full((rows, 1), INT_MIN, jnp.int32))


def _dsa_prompt_kernel(q_ref, k_ref, v_ref, qi_ref, kid_ref, wi_ref, o_ref, key_ref, bias_ref, *, tq, t, topk):
    t0 = pl.program_id(1) * tq
    kid = kid_ref[...]
    wi = wi_ref[...]
    acc = jnp.zeros((tq, t), F32)
    for h in range(IDX_HEADS):
        acc = acc + jnp.maximum(_nt(qi_ref[:, h * LANES:(h + 1) * LANES], kid), 0.0) * wi[:, h:h + 1]
    col = lax.broadcasted_iota(jnp.int32, (tq, t), 1)
    row = t0 + lax.broadcasted_iota(jnp.int32, (tq, t), 0)
    causal = col <= row
    key_ref[...] = _order_key(jnp.where(causal, acc, NEG_INF))

    def count_ge(cand):
        return jnp.sum(jnp.where(key_ref[...] >= cand, 1.0, 0.0), axis=-1, keepdims=True)

    thr = _kth_largest(count_ge, tq, topk)
    bias_ref[...] = jnp.where((key_ref[...] >= thr) & causal, 0.0, NEG_INF)
    scale = ATT_HD ** -0.5
    for h in range(ATT_HEADS):
        sl = slice(h * ATT_HD, (h + 1) * ATT_HD)
        s = _nt(q_ref[:, sl], k_ref[:, sl]) * scale + bias_ref[...]
        mx = jnp.max(s, axis=-1, keepdims=True)
        pr = jnp.exp(s - mx)
        den = jnp.sum(pr, axis=-1, keepdims=True)
        o_ref[:, sl] = jnp.dot(pr.astype(BF16), v_ref[:, sl], preferred_element_type=F32) / den


def _dsa_prompt(qb, kb, vb, qib, kid, wi, b, t):
    topk = min(TOPK_MAX, t // 4)
    tq = min(256, t)
    nq = t // tq
    blk = lambda w: pl.BlockSpec((tq, w), lambda i, j: (i * nq + j, 0))
    full = lambda w: pl.BlockSpec((t, w), lambda i, j: (i, 0))
    return pl.pallas_call(
        functools.partial(_dsa_prompt_kernel, tq=tq, t=t, topk=topk),
        grid=(b, nq),
        in_specs=[blk(ATT_W), full(ATT_W), full(ATT_W), blk(IDX_WX), full(LANES), blk(IDX_HEADS)],
        out_specs=blk(ATT_W),
        out_shape=jax.ShapeDtypeStruct((b * t, ATT_W), F32),
        scratch_shapes=[pltpu.VMEM((tq, t), jnp.int32), pltpu.VMEM((tq, t), F32)],
        compiler_params=_cparams("parallel", "arbitrary"),
        name="dsa_prompt",
    )(qb, kb, vb, qib, kid, wi)


QS = 8
NS = 16


def _dsa_sample_kernel(pt_ref, qbd_ref, knew_ref, vnew_ref, a_ref, wcol_ref, kinew_ref, ck_hbm, cv_hbm, cki_hbm,
                       o_ref, kbuf, vbuf, kibuf, sem, *, n_pages, s_new, topk):
    b = pl.program_id(0)
    nb = pl.num_programs(0)
    slot = b % 2
    past = n_pages * PAGE_SIZE

    def copies(bb, sl):
        out = []
        for p in range(n_pages):
            page = pt_ref[bb, p]
            dst = pl.ds(p * PAGE_SIZE, PAGE_SIZE)
            out.append(pltpu.make_async_copy(ck_hbm.at[page], kbuf.at[sl, dst], sem.at[0, sl]))
            out.append(pltpu.make_async_copy(cv_hbm.at[page], vbuf.at[sl, dst], sem.at[1, sl]))
            out.append(pltpu.make_async_copy(cki_hbm.at[page], kibuf.at[sl, dst], sem.at[2, sl]))
        return out

    @pl.when(b == 0)
    def _():
        for c in copies(0, 0):
            c.start()

    @pl.when(b + 1 < nb)
    def _():
        for c in copies(b + 1, 1 - slot):
            c.start()

    for c in copies(b, slot):
        c.wait()

    a = a_ref[0]
    wcol = wcol_ref[0]
    ki = kibuf[slot].astype(BF16)
    ki = jnp.concatenate([ki, jnp.zeros_like(ki)], axis=1)
    ip = (jnp.maximum(_nt(a, ki), 0.0) * wcol).reshape(IDX_HEADS, QS, past).sum(axis=0)
    inw = (jnp.maximum(_nt(a, kinew_ref[0]), 0.0) * wcol).reshape(IDX_HEADS, QS, NS).sum(axis=0)
    qrow = lax.broadcasted_iota(jnp.int32, (QS, NS), 0)
    jcol = lax.broadcasted_iota(jnp.int32, (QS, NS), 1)
    new_ok = (jcol <= qrow) & (jcol < s_new)
    key_p = _order_key(ip)
    key_n = _order_key(jnp.where(new_ok, inw, NEG_INF))

    def count_ge(cand):
        return (jnp.sum(jnp.where(key_p >= cand, 1.0, 0.0), axis=-1, keepdims=True)
                + jnp.sum(jnp.where(key_n >= cand, 1.0, 0.0), axis=-1, keepdims=True))

    thr = _kth_largest(count_ge, QS, topk)
    bias_p = jnp.where(key_p >= thr, 0.0, NEG_INF)
    bias_n = jnp.where((key_n >= thr) & new_ok, 0.0, NEG_INF)

    scale = ATT_HD ** -0.5
    qbd = qbd_ref[0]
    kb = kbuf[slot].astype(BF16)
    sp = (_nt(qbd, kb) * scale).reshape(ATT_HEADS, QS, past) + bias_p[None]
    sn = (_nt(qbd, knew_ref[0]) * scale).reshape(ATT_HEADS, QS, NS) + bias_n[None]
    mx = jnp.maximum(jnp.max(sp, axis=-1, keepdims=True), jnp.max(sn, axis=-1, keepdims=True))
    pp = jnp.exp(sp - mx)
    pn = jnp.exp(sn - mx)
    den = jnp.sum(pp, axis=-1, keepdims=True) + jnp.sum(pn, axis=-1, keepdims=True)
    vb = vbuf[slot].astype(BF16)
    o = jnp.dot(pp.reshape(ATT_HEADS * QS, past).astype(BF16), vb, preferred_element_type=F32)
    pn2 = pn.reshape(ATT_HEADS * QS, NS)
    vnew = vnew_ref[0]
    for j in range(s_new):
        o = o + pn2[:, j:j + 1] * vnew[j:j + 1, :]
    o = o / den.reshape(ATT_HEADS * QS, 1)
    for h in range(ATT_HEADS):
        o_ref[0, :, h * ATT_HD:(h + 1) * ATT_HD] = o[h * QS:h * QS + s_new, h * ATT_HD:(h + 1) * ATT_HD]


def _dsa_sample(qb, kb, v32, qib, kid, wi, ck, cv, cki, page_table, db, s):
    n_pages = page_table.shape[1]
    past = n_pages * PAGE_SIZE
    topk = min(TOPK_MAX, (past + s) // 4)
    n_pool = ck.shape[0]
    q4 = jnp.pad(qb.reshape(db, s, ATT_HEADS, ATT_HD), ((0, 0), (0, QS - s), (0, 0), (0, 0)))
    eye = jnp.eye(ATT_HEADS, dtype=BF16)
    qbd = (q4.transpose(0, 2, 1, 3)[:, :, :, None, :] * eye[None, :, None, :, None]).reshape(db, ATT_HEADS * QS, ATT_W)
    knew = jnp.pad(kb.reshape(db, s, ATT_W), ((0, 0), (0, NS - s), (0, 0)))
    vnew = jnp.pad(v32.reshape(db, s, ATT_W), ((0, 0), (0, NS - s), (0, 0)))
    a = qib.reshape(db, s, IDX_HEADS, 2, IDX_HD).sum(axis=3)
    a = jnp.pad(a, ((0, 0), (0, QS - s), (0, 0), (0, LANES - IDX_HD)))
    a = a.transpose(0, 2, 1, 3).reshape(db, IDX_HEADS * QS, LANES)
    wcol = jnp.pad(wi.reshape(db, s, IDX_HEADS), ((0, 0), (0, QS - s), (0, 0)))
    wcol = wcol.transpose(0, 2, 1).reshape(db, IDX_HEADS * QS, 1)
    kinew = jnp.pad(kid.reshape(db, s, LANES)[:, :, :IDX_HD], ((0, 0), (0, NS - s), (0, LANES - IDX_HD)))
    per = lambda r, w: pl.BlockSpec((1, r, w), lambda i, pt: (i, 0, 0))
    hbm = pl.BlockSpec(memory_space=pl.ANY)
    grid_spec = pltpu.PrefetchScalarGridSpec(
        num_scalar_prefetch=1,
        grid=(db,),
        in_specs=[per(ATT_HEADS * QS, ATT_W), per(NS, ATT_W), per(NS, ATT_W), per(IDX_HEADS * QS, LANES),
                  per(IDX_HEADS * QS, 1), per(NS, LANES), hbm, hbm, hbm],
        out_specs=per(s, ATT_W),
        scratch_shapes=[pltpu.VMEM((2, past, ATT_W), F32), pltpu.VMEM((2, past, ATT_W), F32),
                        pltpu.VMEM((2, past, IDX_HD), F32), pltpu.SemaphoreType.DMA((3, 2))],
    )
    out = pl.pallas_call(
        functools.partial(_dsa_sample_kernel, n_pages=n_pages, s_new=s, topk=topk),
        grid_spec=grid_spec,
        out_shape=jax.ShapeDtypeStruct((db, s, ATT_W), F32),
        compiler_params=_cparams("arbitrary"),
        name="dsa_sample",
    )(page_table, qbd, knew, vnew, a, wcol, kinew, ck.reshape(n_pool, PAGE_SIZE, ATT_W),
      cv.reshape(n_pool, PAGE_SIZE, ATT_W), cki)
    return out.reshape(db * s, ATT_W)


def _head_ones():
    i = jnp.arange(RW_W) // RW_HD
    return (i[:, None] == i[None, :]).astype(F32)


def _rw_prep_kernel(z_ref, p_ref, mu_ref, w0_ref, a0_ref, kkg_ref, ka_ref, w2_ref, a2_ref, g2_ref, seg_ref,
                    r_ref, lw_ref, k_ref, v_ref, kk_ref, a_ref, g_ref):
    z = z_ref[...]
    xm = z + (p_ref[...] - z) * mu_ref[...]
    r = xm[:, :RW_W]
    k = xm[:, RW_W:2 * RW_W]
    tail = xm[:, 3 * RW_W:]
    dw = jnp.dot(jnp.tanh(tail), w2_ref[...], precision=HI, preferred_element_type=F32)
    da = jnp.dot(tail, a2_ref[...], precision=HI, preferred_element_type=F32)
    lw_ref[...] = -_sigmoid(w0_ref[...] + dw) * math.exp(-0.5)
    a = _sigmoid(a0_ref[...] + da)
    g_ref[...] = jnp.dot(_sigmoid(tail), g2_ref[...], precision=HI, preferred_element_type=F32)
    kk = k * kkg_ref[...]
    ss = jnp.dot(kk * kk, seg_ref[...], precision=HI, preferred_element_type=F32)
    kk_ref[...] = kk / jnp.maximum(jnp.sqrt(ss), 1e-12)
    k_ref[...] = k * (1.0 + (a - 1.0) * ka_ref[...])
    r_ref[...] = r
    v_ref[...] = xm[:, 2 * RW_W:3 * RW_W]
    a_ref[...] = a


def _rw_prep(zr, prev, mu, w0, a0, kkg, ka, w2f, a2f, g2f, seg):
    m = zr.shape[0]
    tm = min(512, m)
    row = lambda w: pl.BlockSpec((tm, w), lambda i: (i, 0))
    const = lambda r, w: pl.BlockSpec((r, w), lambda i: (0, 0))
    return pl.pallas_call(
        _rw_prep_kernel,
        grid=(m // tm,),
        in_specs=[row(RW_IN), row(RW_IN), const(1, RW_IN)] + [const(1, RW_W)] * 4
                 + [const(LANES, RW_W)] * 3 + [const(RW_W, RW_W)],
        out_specs=[row(RW_W)] * 7,
        out_shape=[jax.ShapeDtypeStruct((m, RW_W), F32)] * 7,
        compiler_params=_cparams("parallel"),
        name="rwkv_prep",
    )(zr, prev, mu, w0, a0, kkg, ka, w2f, a2f, g2f, seg)


RW_CHUNK = LANES


def _rw_scan_kernel(r_ref, lw_ref, k_ref, v_ref, kk_ref, a_ref, h0_ref, y_ref, hT_ref, st_ref, *, nchunk):
    c = RW_CHUNK
    npair = RW_HEADS // 2
    tb = pl.program_id(1)

    @pl.when(tb == 0)
    def _():
        st_ref[...] = jnp.zeros_like(st_ref)
        for h in range(RW_HEADS):
            o = (h % 2) * RW_HD
            st_ref[h // 2, o:o + RW_HD, o:o + RW_HD] = h0_ref[0, h]

    ri = lax.broadcasted_iota(jnp.int32, (c, c), 0)
    ci = lax.broadcasted_iota(jnp.int32, (c, c), 1)
    low_incl = ci <= ri
    low_strict = ci < ri
    diag = ci == ri
    tri = jnp.where(low_incl, 1.0, 0.0)
    eye = jnp.where(diag, 1.0, 0.0)
    half0 = ci < RW_HD
    blockdiag = (ri // RW_HD) == (ci // RW_HD)
    nsq = int(math.log2(c))
    mm = functools.partial(jnp.dot, precision=HI, preferred_element_type=F32)

    def chunk(ic, carry):
        rows = pl.ds(pl.multiple_of(ic * c, c), c)
        for p in range(npair):
            lanes = slice(p * LANES, (p + 1) * LANES)
            lw = lw_ref[rows, lanes]
            r = r_ref[rows, lanes]
            k = k_ref[rows, lanes]
            v = v_ref[rows, lanes]
            kk = kk_ref[rows, lanes]
            a = a_ref[rows, lanes]
            cum = mm(tri, lw)
            tot = cum[c - 1:c, :]
            pinv = jnp.exp(-cum)
            abar = -kk * jnp.exp(cum - lw)
            rbar = r * jnp.exp(cum)
            bt = kk * a * pinv
            kt = k * pinv
            pend = jnp.exp(tot - cum)
            bhat = kk * a * pend
            khat = k * pend
            hst = st_ref[p]
            u = None
            y = None
            for hf in range(2):
                msk = half0 if hf == 0 else ~half0
                ah = jnp.where(msk, abar, 0.0)
                rh = jnp.where(msk, rbar, 0.0)
                lab = jnp.where(low_strict, _nt(ah, bt, HI), 0.0)
                lak = jnp.where(low_strict, _nt(ah, kt, HI), 0.0)
                mrb = jnp.where(low_incl, _nt(rh, bt, HI), 0.0)
                mrk = jnp.where(low_incl, _nt(rh, kt, HI), 0.0)
                tinv = eye + lab
                pw = lab
                for _ in range(nsq - 1):
                    pw = mm(pw, pw)
                    tinv = mm(tinv, eye + pw)
                x = mm(jnp.concatenate([ah, lak], axis=1), jnp.concatenate([hst, v], axis=0))
                uh = mm(tinv, x)
                yh = mm(jnp.concatenate([rh, mrb, mrk], axis=1), jnp.concatenate([hst, uh, v], axis=0))
                u = uh if hf == 0 else jnp.where(half0, u, uh)
                y = yh if hf == 0 else jnp.where(half0, y, yh)
            y_ref[rows, lanes] = y
            decay = jnp.where(diag, jnp.exp(tot), 0.0)
            upd = _tn(jnp.concatenate([bhat, khat, decay], axis=0), jnp.concatenate([u, v, hst], axis=0), HI)
            st_ref[p] = jnp.where(blockdiag, upd, 0.0)
        return carry

    lax.fori_loop(0, nchunk, chunk, 0)

    @pl.when(tb == pl.num_programs(1) - 1)
    def _():
        for h in range(RW_HEADS):
            o = (h % 2) * RW_HD
            hT_ref[0, h] = st_ref[h // 2, o:o + RW_HD, o:o + RW_HD]


def _rw_scan(r, lw, k, v, kk, a, s0, b, t):
    tb = min(512, t)
    nt = t // tb
    row = pl.BlockSpec((tb, RW_W), lambda i, j: (i * nt + j, 0))
    st = pl.BlockSpec((1, RW_HEADS, RW_HD, RW_HD), lambda i, j: (i, 0, 0, 0))
    y, h_end = pl.pallas_call(
        functools.partial(_rw_scan_kernel, nchunk=tb // RW_CHUNK),
        grid=(b, nt),
        in_specs=[row] * 6 + [st],
        out_specs=[row, st],
        out_shape=[jax.ShapeDtypeStruct((b * t, RW_W), F32),
                   jax.ShapeDtypeStruct((b, RW_HEADS, RW_HD, RW_HD), F32)],
        scratch_shapes=[pltpu.VMEM((RW_HEADS // 2, LANES, LANES), F32)],
        compiler_params=_cparams("parallel", "arbitrary"),
        name="rwkv_scan",
    )(r, lw, k, v, kk, a, jnp.swapaxes(s0, -1, -2))
    return y, jnp.swapaxes(h_end, -1, -2)


def _rw_post_kernel(y_ref, r_ref, k_ref, v_ref, g_ref, rk_ref, lg_ref, lb_ref, seg_ref, o_ref):
    seg = seg_ref[...]
    y = y_ref[...]
    mean = jnp.dot(y, seg, precision=HI, preferred_element_type=F32) * (1.0 / RW_HD)
    yc = y - mean
    var = jnp.dot(yc * yc, seg, precision=HI, preferred_element_type=F32) * (1.0 / RW_HD)
    yn = yc * lax.rsqrt(var + RW_LN_EPS) * lg_ref[...] + lb_ref[...]
    v = v_ref[...]
    bonus = jnp.dot(r_ref[...] * k_ref[...] * rk_ref[...], seg, precision=HI, preferred_element_type=F32) * v
    o_ref[...] = ((yn + bonus) * g_ref[...]).astype(BF16)


def _rw_post(y, r, k, v, g, rk, lg, lb, seg):
    m = y.shape[0]
    tm = min(512, m)
    row = pl.BlockSpec((tm, RW_W), lambda i: (i, 0))
    vec = pl.BlockSpec((1, RW_W), lambda i: (0, 0))
    return pl.pallas_call(
        _rw_post_kernel,
        grid=(m // tm,),
        in_specs=[row] * 5 + [vec] * 3 + [pl.BlockSpec((RW_W, RW_W), lambda i: (0, 0))],
        out_specs=row,
        out_shape=jax.ShapeDtypeStruct((m, RW_W), BF16),
        compiler_params=_cparams("parallel"),
        name="rwkv_post",
    )(y, r, k, v, g, rk, lg, lb, seg)


def _s5_mats(lam_re, lam_im, log_step, b_re, b_im, c_re, c_im, cs):
    delta = jnp.exp(log_step)[:, None]
    mag = jnp.exp(lam_re * delta)
    ab_re, ab_im = mag * jnp.cos(lam_im * delta), mag * jnp.sin(lam_im * delta)
    den = lam_re * lam_re + lam_im * lam_im
    n_re, n_im = ab_re - 1.0, ab_im
    q_re = (n_re * lam_re + n_im * lam_im) / den
    q_im = (n_im * lam_re - n_re * lam_im) / den
    bb_re = q_re[..., None] * b_re - q_im[..., None] * b_im
    bb_im = q_re[..., None] * b_im + q_im[..., None] * b_re
    pr, pi = [jnp.ones_like(ab_re)], [jnp.zeros_like(ab_im)]
    for _ in range(cs):
        pr, pi = pr + [pr[-1] * ab_re - pi[-1] * ab_im], pi + [pr[-1] * ab_im + pi[-1] * ab_re]
    pr, pi = jnp.stack(pr), jnp.stack(pi)
    wr = pr[:cs, :, :, None] * bb_re[None] - pi[:cs, :, :, None] * bb_im[None]
    wi = pr[:cs, :, :, None] * bb_im[None] + pi[:cs, :, :, None] * bb_re[None]
    kern = (jnp.einsum('gcp,tgpd->tgdc', c_re, wr, precision=HI)
            - jnp.einsum('gcp,tgpd->tgdc', c_im, wi, precision=HI))
    lag = jnp.arange(cs)[None, :] - jnp.arange(cs)[:, None]
    toep = jnp.where((lag >= 0)[:, :, None, None, None], kern[jnp.maximum(lag, 0)], 0.0)
    toep = toep.transpose(2, 0, 3, 1, 4).reshape(S5_GROUPS, cs * S5_GC, cs * S5_GC)
    rev = jnp.arange(cs - 1, -1, -1)
    gmat = jnp.concatenate([wr[rev], wi[rev]], axis=2)
    gmat = gmat.transpose(1, 0, 3, 2).reshape(S5_GROUPS, cs * S5_GC, 2 * S5_P)
    p1r, p1i = pr[1:], pi[1:]
    h_re = c_re[None] * p1r[:, :, None, :] - c_im[None] * p1i[:, :, None, :]
    h_im = -c_re[None] * p1i[:, :, None, :] - c_im[None] * p1r[:, :, None, :]
    hmat = jnp.concatenate([h_re, h_im], axis=3)
    hmat = hmat.transpose(1, 3, 0, 2).reshape(S5_GROUPS, 2 * S5_P, cs * S5_GC)
    lam_a = jnp.concatenate([pr[cs], pr[cs]], axis=-1)[:, None, :]
    lam_b = jnp.concatenate([-pi[cs], pi[cs]], axis=-1)[:, None, :]
    return toep, gmat, hmat, lam_a, lam_b


def _s5_in_kernel(u_ref, gm_ref, o_ref):
    o_ref[0] = jnp.dot(u_ref[0], gm_ref[0], precision=HI, preferred_element_type=F32)


def _s5_carry_kernel(g_ref, x0_ref, la_ref, lb_ref, xs_ref, xT_ref, *, nj):
    gt, bsz = x0_ref.shape[0], x0_ref.shape[1]
    la = jnp.broadcast_to(la_ref[...], (gt, bsz, LANES)).reshape(gt * bsz, LANES)
    lb = jnp.broadcast_to(lb_ref[...], (gt, bsz, LANES)).reshape(gt * bsz, LANES)

    def step(j, x):
        xs_ref[:, j] = x.reshape(gt, bsz, LANES)
        return x * la + pltpu.roll(x, S5_P, 1) * lb + g_ref[:, j].reshape(gt * bsz, LANES)

    x = lax.fori_loop(0, nj, step, x0_ref[...].reshape(gt * bsz, LANES))
    xT_ref[...] = x.reshape(gt, bsz, LANES)


def _s5_out_kernel(u_ref, xs_ref, tp_ref, hm_ref, d_ref, o_ref):
    u = u_ref[0]
    y = (jnp.dot(u, tp_ref[0], precision=HI, preferred_element_type=F32)
         + jnp.dot(xs_ref[0], hm_ref[0], precision=HI, preferred_element_type=F32))
    o_ref[0] = y + d_ref[0] * u


def _s5_scan(ug, x0, mats, dg, nj, bsz):
    toep, gmat, hmat, lam_a, lam_b = mats
    ng, rows, w = ug.shape
    grp = lambda r, c: pl.BlockSpec((1, r, c), lambda g: (g, 0, 0))
    gin = pl.pallas_call(
        _s5_in_kernel,
        grid=(ng,),
        in_specs=[grp(rows, w), grp(w, LANES)],
        out_specs=grp(rows, LANES),
        out_shape=jax.ShapeDtypeStruct((ng, rows, LANES), F32),
        compiler_params=_cparams("parallel"),
        name="s5_chunk_in",
    )(ug, gmat)
    gt = 8
    xs, x_last = pl.pallas_call(
        functools.partial(_s5_carry_kernel, nj=nj),
        grid=(ng // gt,),
        in_specs=[pl.BlockSpec((gt, nj, bsz, LANES), lambda g: (g, 0, 0, 0)),
                  pl.BlockSpec((gt, bsz, LANES), lambda g: (g, 0, 0)),
                  pl.BlockSpec((gt, 1, LANES), lambda g: (g, 0, 0)),
                  pl.BlockSpec((gt, 1, LANES), lambda g: (g, 0, 0))],
        out_specs=[pl.BlockSpec((gt, nj, bsz, LANES), lambda g: (g, 0, 0, 0)),
                   pl.BlockSpec((gt, bsz, LANES), lambda g: (g, 0, 0))],
        out_shape=[jax.ShapeDtypeStruct((ng, nj, bsz, LANES), F32),
                   jax.ShapeDtypeStruct((ng, bsz, LANES), F32)],
        compiler_params=_cparams("parallel"),
        name="s5_carry",
    )(gin.reshape(ng, nj, bsz, LANES), x0, lam_a, lam_b)
    yg = pl.pallas_call(
        _s5_out_kernel,
        grid=(ng,),
        in_specs=[grp(rows, w), grp(rows, LANES), grp(w, w), grp(LANES, w), grp(1, w)],
        out_specs=grp(rows, w),
        out_shape=jax.ShapeDtypeStruct((ng, rows, w), F32),
        compiler_params=_cparams("parallel"),
        name="s5_chunk_out",
    )(ug, xs.reshape(ng, rows, LANES), toep, hmat, dg)
    return yg, x_last


def _s5_glu_kernel(y_ref, w1_ref, w2_ref, o_ref):
    y = y_ref[...]
    y = y * (0.5 * (1.0 + jnp.tanh(math.sqrt(2.0 / math.pi) * (y + 0.044715 * (y * y * y)))))
    yb = y.astype(BF16)
    o_ref[...] = (jnp.dot(yb, w1_ref[...], preferred_element_type=F32)
                  * _sigmoid(jnp.dot(yb, w2_ref[...], preferred_element_type=F32))).astype(BF16)


def _s5_glu(y, w1, w2):
    m = y.shape[0]
    tm = min(512, m)
    row = pl.BlockSpec((tm, S5_W), lambda i: (i, 0))
    mat = pl.BlockSpec((S5_W, S5_W), lambda i: (0, 0))
    return pl.pallas_call(
        _s5_glu_kernel,
        grid=(m // tm,),
        in_specs=[row, mat, mat],
        out_specs=row,
        out_shape=jax.ShapeDtypeStruct((m, S5_W), BF16),
        compiler_params=_cparams("parallel"),
        name="s5_glu",
    )(y, w1, w2)


def _s5(zs, x0_re, x0_im, mats, dskip, glu1, glu2, b, t, cs):
    nj = t // cs
    w = cs * S5_GC
    wp = max(w, LANES)
    lpad = lambda m, axes: jnp.pad(m, [(0, wp - w) if ax in axes else (0, 0) for ax in range(m.ndim)])
    toep, gmat, hmat, lam_a, lam_b = mats
    mats = (lpad(toep, (1, 2)), lpad(gmat, (1,)), lpad(hmat, (2,)), lam_a, lam_b)
    ug = zs.reshape(b, nj, cs, S5_GROUPS, S5_GC).transpose(3, 1, 0, 2, 4).reshape(S5_GROUPS, nj * b, w)
    x0 = jnp.concatenate([x0_re, x0_im], axis=-1).transpose(1, 0, 2)
    dg = jnp.tile(dskip.reshape(S5_GROUPS, 1, S5_GC), (1, 1, cs))
    yg, x_last = _s5_scan(lpad(ug, (2,)), x0, mats, lpad(dg, (2,)), nj, b)
    y = yg[:, :, :w].reshape(S5_GROUPS, nj, b, cs, S5_GC).transpose(2, 1, 3, 0, 4).reshape(b * t, S5_W)
    x_last = x_last.transpose(1, 0, 2)
    return _s5_glu(y, glu1, glu2), x_last[..., :S5_P], x_last[..., S5_P:]


def _mix_out_kernel(x_ref, ya_ref, yr_ref, ys_ref, w_ref, o_ref):
    acc = jnp.dot(ya_ref[...].astype(BF16), w_ref[:ATT_W, :], preferred_element_type=F32)
    acc = acc + jnp.dot(yr_ref[...], w_ref[ATT_W:ATT_W + RW_W, :], preferred_element_type=F32)
    acc = acc + jnp.dot(ys_ref[...], w_ref[ATT_W + RW_W:, :], preferred_element_type=F32)
    o_ref[...] = x_ref[...] + acc


def _mix_out(x, ya, yr, ys, w):
    m, d = x.shape
    tm = min(512, m)
    row = lambda c: pl.BlockSpec((tm, c), lambda i: (i, 0))
    return pl.pallas_call(
        _mix_out_kernel,
        grid=(m // tm,),
        in_specs=[row(d), row(ATT_W), row(RW_W), row(S5_W), pl.BlockSpec(w.shape, lambda i: (0, 0))],
        out_specs=row(d),
        out_shape=jax.ShapeDtypeStruct((m, d), F32),
        compiler_params=_cparams("parallel"),
        name="mix_out",
    )(x, ya, yr, ys, w)


def _layer_weights(p, l):
    ffpad = lambda w: jnp.pad(w.astype(BF16), ((0, 0), (0, D_FF_PAD - D_FF)))
    w_in = p['w_in'][l]
    w = {}
    for n in ('ffn1', 'ffn2'):
        w[n] = (p[n + '_norm'][l][None], ffpad(p[n + '_wg'][l]), ffpad(p[n + '_wu'][l]),
                jnp.pad(p[n + '_wd'][l].astype(BF16), ((0, D_FF_PAD - D_FF), (0, 0))))
    w['mix_norm'] = p['mix_norm'][l][None]
    w['w_att'] = jnp.pad(w_in[:, :ATT_IN].astype(BF16), ((0, 0), (0, ATT_Z - ATT_IN)))
    w['w_rw'] = w_in[:, ATT_IN:ATT_IN + RW_IN].astype(BF16)
    w['w_s5'] = w_in[:, ATT_IN + RW_IN:].astype(BF16)
    w['q_norm'] = p['q_norm'][l][None]
    w['k_norm'] = p['k_norm'][l][None]
    tailpad = lambda m, off: jnp.pad(m, ((off, LANES - off - m.shape[0]), (0, 0)))
    w['rw_prep'] = (p['rw_mu'][l][None], p['rw_w0'][l][None], p['rw_a0'][l][None], p['rw_kk'][l][None],
                    p['rw_ka'][l][None], tailpad(p['rw_w2'][l], 0), tailpad(p['rw_a2'][l], RW_DECAY_R),
                    tailpad(p['rw_g2'][l], RW_DECAY_R + RW_A_R), _head_ones())
    w['rw_post'] = (p['rw_rk'][l].reshape(1, RW_W), p['rw_lnx_g'][l][None], p['rw_lnx_b'][l][None], _head_ones())
    w['s5_params'] = (p['s5_lam_re'][l], p['s5_lam_im'][l], p['s5_log_step'][l], p['s5_b_re'][l], p['s5_b_im'][l],
                      p['s5_c_re'][l], p['s5_c_im'][l])
    w['s5_d'] = p['s5_d'][l]
    w['s5_glu'] = (p['s5_glu1'][l].astype(BF16), p['s5_glu2'][l].astype(BF16))
    w['w_out'] = p['w_out'][l].astype(BF16)
    return w


def _layer(x, w, tabs, b, t, attn_fn, shift0, rw0, s5re0, s5im0, s5_chunk):
    m = b * t
    x = _ffn(x, *w['ffn1'])
    za = _nmm(x, w['mix_norm'], w['w_att'], ATT_Z // 3)
    zr = _nmm(x, w['mix_norm'], w['w_rw'], RW_IN)
    zs = _nmm(x, w['mix_norm'], w['w_s5'], S5_W)
    qb, k32, kb, vb, qib, ki32, kid, wi = _attn_prep(za, tabs, w['q_norm'], w['k_norm'])
    v32 = za[:, 2 * ATT_W:3 * ATT_W]
    ya = attn_fn(qb, kb, vb, v32, qib, kid, wi)
    zr3 = zr.reshape(b, t, RW_IN)
    prev = jnp.concatenate([shift0[:, None, :], zr3[:, :-1]], axis=1).reshape(m, RW_IN)
    r, lw, k2, v, kk, a, g = _rw_prep(zr, prev, *w['rw_prep'])
    tp = -(-t // RW_CHUNK) * RW_CHUNK
    padt = lambda u: jnp.pad(u.reshape(b, t, RW_W), ((0, 0), (0, tp - t), (0, 0))).reshape(b * tp, RW_W)
    y, rw_new = _rw_scan(padt(r), padt(lw), padt(k2), padt(v), padt(kk), padt(a), rw0, b, tp)
    y = y.reshape(b, tp, RW_W)[:, :t].reshape(m, RW_W)
    yr = _rw_post(y, r, k2, v, g, *w['rw_post'])
    mats = _s5_mats(*w['s5_params'], s5_chunk)
    ys, s5re, s5im = _s5(zs, s5re0, s5im0, mats, w['s5_d'], *w['s5_glu'], b, t, s5_chunk)
    x = _mix_out(x, ya, yr, ys, w['w_out'])
    x = _ffn(x, *w['ffn2'])
    state = (k32.reshape(b, t, ATT_HEADS, ATT_HD), v32.reshape(b, t, ATT_HEADS, ATT_HD),
             ki32.reshape(b, t, IDX_HD), rw_new, zr3[:, -1], s5re, s5im)
    return x, state


def kernel(x_prompt, x_sample, cache_k, cache_v, cache_kidx, state_rwkv, state_shift, state_s5_re, state_s5_im, page_table, ffn1_norm, ffn1_wg, ffn1_wu, ffn1_wd, mix_norm, w_in, q_norm, k_norm, rw_mu, rw_w0, rw_w2, rw_a0, rw_a2, rw_g2, rw_kk, rw_ka, rw_rk, rw_lnx_g, rw_lnx_b, s5_lam_re, s5_lam_im, s5_log_step, s5_b_re, s5_b_im, s5_c_re, s5_c_im, s5_d, s5_glu1, s5_glu2, w_out, ffn2_norm, ffn2_wg, ffn2_wu, ffn2_wd):
    p = dict(ffn1_norm=ffn1_norm, ffn1_wg=ffn1_wg, ffn1_wu=ffn1_wu, ffn1_wd=ffn1_wd, mix_norm=mix_norm, w_in=w_in,
             q_norm=q_norm, k_norm=k_norm, rw_mu=rw_mu, rw_w0=rw_w0, rw_w2=rw_w2, rw_a0=rw_a0, rw_a2=rw_a2,
             rw_g2=rw_g2, rw_kk=rw_kk, rw_ka=rw_ka, rw_rk=rw_rk, rw_lnx_g=rw_lnx_g, rw_lnx_b=rw_lnx_b,
             s5_lam_re=s5_lam_re, s5_lam_im=s5_lam_im, s5_log_step=s5_log_step, s5_b_re=s5_b_re, s5_b_im=s5_b_im,
             s5_c_re=s5_c_re, s5_c_im=s5_c_im, s5_d=s5_d, s5_glu1=s5_glu1, s5_glu2=s5_glu2, w_out=w_out,
             ffn2_norm=ffn2_norm, ffn2_wg=ffn2_wg, ffn2_wu=ffn2_wu, ffn2_wd=ffn2_wd)
    depth = w_in.shape[0]
    b, t, d = x_prompt.shape
    db, s, _ = x_sample.shape
    past = page_table.shape[1] * PAGE_SIZE
    weights = [_layer_weights(p, l) for l in range(depth)]

    tabs_p = _rope_tables(jnp.arange(t))
    attn_p = lambda qb, kb, vb, v32, qib, kid, wi: _dsa_prompt(qb, kb, vb, qib, kid, wi, b, t)
    y = x_prompt.reshape(b * t, d)
    st_p = []
    for l in range(depth):
        y, st = _layer(y, weights[l], tabs_p, b, t, attn_p, jnp.zeros((b, RW_IN), F32),
                       jnp.zeros((b, RW_HEADS, RW_HD, RW_HD), F32), jnp.zeros((b, S5_GROUPS, S5_P), F32),
                       jnp.zeros((b, S5_GROUPS, S5_P), F32), s5_chunk=16)
        st_p.append(st)
    y_prompt = y.reshape(b, t, d)

    ms = db * s
    tm_s = min(512, ms)
    tabs_s = tuple(jnp.tile(tb, (tm_s // s, 1)) for tb in _rope_tables(past + jnp.arange(s)))
    y = x_sample.reshape(ms, d)
    st_s = []
    for l in range(depth):
        attn_s = functools.partial(
            lambda qb, kb, vb, v32, qib, kid, wi, l: _dsa_sample(qb, kb, v32, qib, kid, wi, cache_k[l], cache_v[l],
                                                                  cache_kidx[l], page_table, db, s), l=l)
        y, st = _layer(y, weights[l], tabs_s, db, s, attn_s, state_shift[l], state_rwkv[l], state_s5_re[l],
                       state_s5_im[l], s5_chunk=s)
        st_s.append(st)
    y_sample = y.reshape(db, s, d)
    outs_p = [jnp.stack(u) for u in zip(*st_p)]
    outs_s = [jnp.stack(u) for u in zip(*st_s)]
    return (y_prompt, y_sample, *outs_p, *outs_s)
```

```python
import functools
import math

import jax
import jax.numpy as jnp
from jax import lax
from jax.experimental import pallas as pl
from jax.experimental.pallas import tpu as pltpu

F32 = jnp.float32
BF16 = jnp.bfloat16
HI = lax.Precision.HIGHEST

D_MODEL = 2048
PAGE_SIZE = 128
ATT_HD = 128
ATT_W = 1024
ATT_HEADS = 8
IDX_HEADS = 16
IDX_HD = 64
IDX_W = IDX_HEADS * IDX_HD
IDX_WX = IDX_HEADS * 128
TOPK_MAX = 256
ROPE_THETA = 10000.0
RW_HD = 64
RW_W = 512
RW_HEADS = 8
RW_DECAY_R = 32
RW_A_R = 32
RW_G_R = 64
RW_IN = 3 * RW_W + RW_DECAY_R + RW_A_R + RW_G_R
S5_W = 512
S5_GC = 16
S5_GROUPS = 32
S5_P = 64
ATT_IN = 3 * ATT_W + IDX_W + IDX_HD + IDX_HEADS
D_FF = 5504
NORM_EPS = 1e-6
RW_LN_EPS = 64e-5

LANES = 128
VMEM_LIMIT = 52 * 1024 * 1024
FF_TILE = 512
D_FF_PAD = ((D_FF + FF_TILE - 1) // FF_TILE) * FF_TILE
ATT_Z = 4 * ATT_W + LANES
INT_MIN = -2147483648
NEG_INF = float("-inf")


def _cparams(*sem):
    return pltpu.CompilerParams(dimension_semantics=sem, vmem_limit_bytes=VMEM_LIMIT)


def _nt(a, b, precision=None):
    return lax.dot_general(a, b, (((1,), (1,)), ((), ())), precision=precision,
                           preferred_element_type=F32)


def _tn(a, b, precision=None):
    return lax.dot_general(a, b, (((0,), (0,)), ((), ())), precision=precision,
                           preferred_element_type=F32)


def _split(a):
    hi = a.astype(BF16)
    return hi, (a - hi.astype(F32)).astype(BF16)


def _cat(parts, axis):
    return tuple(jnp.concatenate([p[i] for p in parts], axis=axis) for i in range(2))


def _dot3(a, b, dims):
    dg = lambda x, y: lax.dot_general(x, y, dims, preferred_element_type=F32)
    return dg(a[0], b[0]) + dg(a[0], b[1]) + dg(a[1], b[0])


def _sigmoid(x):
    return 1.0 / (1.0 + jnp.exp(-x))


def _rms(x, g):
    ms = jnp.mean(x * x, axis=-1, keepdims=True)
    return x * lax.rsqrt(ms + NORM_EPS) * g


def _ffn_kernel(x_ref, g_ref, wg_ref, wu_ref, wd_ref, o_ref, xn_ref):
    j = pl.program_id(1)

    @pl.when(j == 0)
    def _():
        xn_ref[...] = _rms(x_ref[...], g_ref[...]).astype(BF16)
        o_ref[...] = jnp.zeros_like(o_ref)

    xn = xn_ref[...]
    hg = jnp.dot(xn, wg_ref[...], preferred_element_type=F32)
    hu = jnp.dot(xn, wu_ref[...], preferred_element_type=F32)
    h = (hg * _sigmoid(hg) * hu).astype(BF16)
    o_ref[...] += jnp.dot(h, wd_ref[...], preferred_element_type=F32)

    @pl.when(j == pl.num_programs(1) - 1)
    def _():
        o_ref[...] = x_ref[...] + 0.5 * o_ref[...]


def _ffn(x, g, wg, wu, wd):
    m, d = x.shape
    tm = min(512, m)
    nf = wg.shape[1] // FF_TILE
    return pl.pallas_call(
        _ffn_kernel,
        grid=(m // tm, nf),
        in_specs=[
            pl.BlockSpec((tm, d), lambda i, j: (i, 0)),
            pl.BlockSpec((1, d), lambda i, j: (0, 0)),
            pl.BlockSpec((d, FF_TILE), lambda i, j: (0, j)),
            pl.BlockSpec((d, FF_TILE), lambda i, j: (0, j)),
            pl.BlockSpec((FF_TILE, d), lambda i, j: (j, 0)),
        ],
        out_specs=pl.BlockSpec((tm, d), lambda i, j: (i, 0)),
        out_shape=jax.ShapeDtypeStruct((m, d), F32),
        scratch_shapes=[pltpu.VMEM((tm, d), BF16)],
        compiler_params=_cparams("parallel", "arbitrary"),
        name="ffn",
    )(x, g, wg, wu, wd)


def _nmm_kernel(x_ref, g_ref, w_ref, o_ref, xn_ref):
    @pl.when(pl.program_id(1) == 0)
    def _():
        xn_ref[...] = _rms(x_ref[...], g_ref[...]).astype(BF16)

    o_ref[...] = jnp.dot(xn_ref[...], w_ref[...], preferred_element_type=F32)


def _nmm(x, g, w, tn):
    m, d = x.shape
    n = w.shape[1]
    tm = min(512, m)
    return pl.pallas_call(
        _nmm_kernel,
        grid=(m // tm, n // tn),
        in_specs=[
            pl.BlockSpec((tm, d), lambda i, j: (i, 0)),
            pl.BlockSpec((1, d), lambda i, j: (0, 0)),
            pl.BlockSpec((d, tn), lambda i, j: (0, j)),
        ],
        out_specs=pl.BlockSpec((tm, tn), lambda i, j: (i, j)),
        out_shape=jax.ShapeDtypeStruct((m, n), F32),
        scratch_shapes=[pltpu.VMEM((tm, d), BF16)],
        compiler_params=_cparams("parallel", "arbitrary"),
        name="norm_proj",
    )(x, g, w)


def _rope_tables(pos):
    def tab(half, reps):
        inv = ROPE_THETA ** (-jnp.arange(half, dtype=F32) / half)
        ang = pos.astype(F32)[:, None] * inv[None, :]
        c, s = jnp.cos(ang), jnp.sin(ang)
        return jnp.tile(jnp.concatenate([c, c], -1), (1, reps)), jnp.tile(jnp.concatenate([-s, s], -1), (1, reps))
    c128, s128 = tab(ATT_HD // 2, 1)
    c64, s64 = tab(IDX_HD // 2, 2)
    return c128, s128, c64, s64


def _rope128(x, c, s):
    return x * c + pltpu.roll(x, ATT_HD // 2, 1) * s


def _rope64(x, c, s):
    lane = lax.broadcasted_iota(jnp.int32, x.shape, 1)
    first = (lane % IDX_HD) < IDX_HD // 2
    partner = jnp.where(first, pltpu.roll(x, LANES - IDX_HD // 2, 1), pltpu.roll(x, IDX_HD // 2, 1))
    return x * c + partner * s


def _prep_kernel(q_ref, k_ref, v_ref, qi_ref, kw_ref, c128_ref, s128_ref, c64_ref, s64_ref, qn_ref, kn_ref,
                 qo_ref, ko_ref, kb_ref, vb_ref, qio_ref, kio_ref, kid_ref, wio_ref):
    c128, s128 = c128_ref[...], s128_ref[...]
    c64, s64 = c64_ref[...], s64_ref[...]
    for h in range(ATT_HEADS):
        sl = slice(h * ATT_HD, (h + 1) * ATT_HD)
        q = _rope128(_rms(q_ref[:, sl], qn_ref[...]), c128, s128)
        k = _rope128(_rms(k_ref[:, sl], kn_ref[...]), c128, s128)
        qo_ref[:, sl] = q.astype(BF16)
        ko_ref[:, sl] = k
        kb_ref[:, sl] = k.astype(BF16)
    vb_ref[...] = v_ref[...].astype(BF16)
    lane = lax.broadcasted_iota(jnp.int32, c64.shape, 1)
    for p in range(IDX_W // LANES):
        qi = _rope64(qi_ref[:, p * LANES:(p + 1) * LANES], c64, s64)
        qio_ref[:, 2 * p * LANES:(2 * p + 1) * LANES] = jnp.where(lane < IDX_HD, qi, 0.0).astype(BF16)
        qio_ref[:, (2 * p + 1) * LANES:(2 * p + 2) * LANES] = jnp.where(lane < IDX_HD, 0.0, qi).astype(BF16)
    kw = kw_ref[...]
    ki = _rope64(kw, c64, s64)
    kio_ref[...] = ki[:, :IDX_HD]
    kid_ref[...] = jnp.where(lane < IDX_HD, ki, pltpu.roll(ki, IDX_HD, 1)).astype(BF16)
    wio_ref[...] = kw[:, IDX_HD:IDX_HD + IDX_HEADS] * (IDX_W ** -0.5)


def _attn_prep(za, tabs, q_norm, k_norm):
    m = za.shape[0]
    tm = min(512, m)
    nt = tabs[0].shape[0] // tm
    wide = lambda c: pl.BlockSpec((tm, ATT_W), lambda i: (i, c))
    tab = pl.BlockSpec((tm, LANES), lambda i: (i % nt, 0))
    vec = pl.BlockSpec((1, ATT_HD), lambda i: (0, 0))
    row = lambda w: pl.BlockSpec((tm, w), lambda i: (i, 0))
    sds = lambda w, dt: jax.ShapeDtypeStruct((m, w), dt)
    return pl.pallas_call(
        _prep_kernel,
        grid=(m // tm,),
        in_specs=[wide(0), wide(1), wide(2), wide(3),
                  pl.BlockSpec((tm, LANES), lambda i: (i, 4 * ATT_W // LANES)),
                  tab, tab, tab, tab, vec, vec],
        out_specs=[row(ATT_W), row(ATT_W), row(ATT_W), row(ATT_W), row(IDX_WX), row(IDX_HD), row(LANES),
                   row(IDX_HEADS)],
        out_shape=[sds(ATT_W, BF16), sds(ATT_W, F32), sds(ATT_W, BF16), sds(ATT_W, BF16), sds(IDX_WX, BF16),
                   sds(IDX_HD, F32), sds(LANES, BF16), sds(IDX_HEADS, F32)],
        compiler_params=_cparams("parallel"),
        name="attn_prep",
    )(za, za, za, za, za, *tabs, q_norm, k_norm)


def _order_key(x):
    bits = lax.bitcast_convert_type(x, jnp.int32)
    return bits ^ ((bits >> 31) & jnp.int32(0x7FFFFFFF))


def _kth_largest(count_ge, rows, k):
    def body(i, cur):
        cand = cur + (jnp.int32(1) << (31 - i))
        return jnp.where(count_ge(cand) >= k, cand, cur)
    return lax.fori_loop(0, 32, body, jnp.full((rows, 1), INT_MIN, jnp.int32))


def _dsa_prompt_kernel(q_ref, k_ref, v_ref, qi_ref, kid_ref, wi_ref, o_ref, key_ref, bias_ref, *, tq, t0, topk):
    tk = t0 + tq
    kid = kid_ref[0]
    wi = wi_ref[...]
    acc = jnp.zeros((tq, tk), F32)
    for h in range(IDX_HEADS):
        acc = acc + jnp.maximum(_nt(qi_ref[:, h * LANES:(h + 1) * LANES], kid), 0.0) * wi[:, h:h + 1]
    col = lax.broadcasted_iota(jnp.int32, (tq, tk), 1)
    row = t0 + lax.broadcasted_iota(jnp.int32, (tq, tk), 0)
    causal = col <= row
    key_ref[...] = _order_key(jnp.where(causal, acc, NEG_INF))

    def count_ge(cand):
        return jnp.sum(jnp.where(key_ref[...] >= cand, 1.0, 0.0), axis=-1, keepdims=True)

    thr = _kth_largest(count_ge, tq, topk)
    bias_ref[...] = jnp.where((key_ref[...] >= thr) & causal, 0.0, NEG_INF)
    scale = ATT_HD ** -0.5
    for h in range(ATT_HEADS):
        sl = slice(h * ATT_HD, (h + 1) * ATT_HD)
        s = _nt(q_ref[:, sl], k_ref[0, :, sl]) * scale + bias_ref[...]
        mx = jnp.max(s, axis=-1, keepdims=True)
        pr = jnp.exp(s - mx)
        den = jnp.sum(pr, axis=-1, keepdims=True)
        o_ref[:, sl] = jnp.dot(pr.astype(BF16), v_ref[0, :, sl], preferred_element_type=F32) / den


def _dsa_prompt(qb, kb, vb, qib, kid, wi, b, t):
    topk = min(TOPK_MAX, t // 4)
    tq = min(256, t)
    nq = t // tq
    k3, v3, kid3 = kb.reshape(b, t, ATT_W), vb.reshape(b, t, ATT_W), kid.reshape(b, t, LANES)
    outs = []
    for j in range(nq):
        tk = (j + 1) * tq
        blk = lambda w, j=j: pl.BlockSpec((tq, w), lambda i: (i * nq + j, 0))
        keys = lambda w, tk=tk: pl.BlockSpec((1, tk, w), lambda i: (i, 0, 0))
        outs.append(pl.pallas_call(
            functools.partial(_dsa_prompt_kernel, tq=tq, t0=j * tq, topk=topk),
            grid=(b,),
            in_specs=[blk(ATT_W), keys(ATT_W), keys(ATT_W), blk(IDX_WX), keys(LANES), blk(IDX_HEADS)],
            out_specs=pl.BlockSpec((tq, ATT_W), lambda i: (i, 0)),
            out_shape=jax.ShapeDtypeStruct((b * tq, ATT_W), F32),
            scratch_shapes=[pltpu.VMEM((tq, tk), jnp.int32), pltpu.VMEM((tq, tk), F32)],
            compiler_params=_cparams("parallel"),
            name="dsa_prompt",
        )(qb, k3, v3, qib, kid3, wi))
    return jnp.stack([o.reshape(b, tq, ATT_W) for o in outs], axis=1).reshape(b * t, ATT_W)


QS = 8
NS = 16


def _dsa_sample_kernel(pt_ref, q_ref, knew_ref, vnew_ref, a_ref, wcol_ref, kinew_ref, ck_hbm, cv_hbm, cki_hbm,
                       o_ref, kbuf, vbuf, kibuf, sem, *, layer, n_pages, s_new, topk):
    b = pl.program_id(0)
    nb = pl.num_programs(0)
    slot = b % 2
    past = n_pages * PAGE_SIZE

    def copies(bb, sl):
        out = []
        for p in range(n_pages):
            page = pt_ref[bb, p]
            dst = pl.ds(p * PAGE_SIZE, PAGE_SIZE)
            out.append(pltpu.make_async_copy(ck_hbm.at[layer, page], kbuf.at[sl, dst], sem.at[0, sl]))
            out.append(pltpu.make_async_copy(cv_hbm.at[layer, page], vbuf.at[sl, dst], sem.at[1, sl]))
            out.append(pltpu.make_async_copy(cki_hbm.at[layer, page], kibuf.at[sl, dst], sem.at[2, sl]))
        return out

    @pl.when(b == 0)
    def _():
        for c in copies(0, 0):
            c.start()

    @pl.when(b + 1 < nb)
    def _():
        for c in copies(b + 1, 1 - slot):
            c.start()

    for c in copies(b, slot):
        c.wait()

    a = a_ref[0]
    wcol = wcol_ref[0]
    ki = kibuf[slot].astype(BF16)
    ki = jnp.concatenate([ki, jnp.zeros_like(ki)], axis=1)
    ip = (jnp.maximum(_nt(a, ki), 0.0) * wcol).reshape(IDX_HEADS, QS, past).sum(axis=0)
    inw = (jnp.maximum(_nt(a, kinew_ref[0]), 0.0) * wcol).reshape(IDX_HEADS, QS, NS).sum(axis=0)
    qrow = lax.broadcasted_iota(jnp.int32, (QS, NS), 0)
    jcol = lax.broadcasted_iota(jnp.int32, (QS, NS), 1)
    new_ok = (jcol <= qrow) & (jcol < s_new)
    key_p = _order_key(ip)
    key_n = _order_key(jnp.where(new_ok, inw, NEG_INF))

    def count_ge(cand):
        return (jnp.sum(jnp.where(key_p >= cand, 1.0, 0.0), axis=-1, keepdims=True)
                + jnp.sum(jnp.where(key_n >= cand, 1.0, 0.0), axis=-1, keepdims=True))

    thr = _kth_largest(count_ge, QS, topk)
    bias_p = jnp.where(key_p >= thr, 0.0, NEG_INF)
    bias_n = jnp.where((key_n >= thr) & new_ok, 0.0, NEG_INF)

    scale = ATT_HD ** -0.5
    for h in range(ATT_HEADS):
        qh = q_ref[0, h]
        sp = _nt(qh, kbuf[slot, :, h, :].astype(BF16)) * scale + bias_p
        sn = _nt(qh, knew_ref[0, h]) * scale + bias_n
        mx = jnp.maximum(jnp.max(sp, axis=-1, keepdims=True), jnp.max(sn, axis=-1, keepdims=True))
        pp = jnp.exp(sp - mx)
        pn = jnp.exp(sn - mx)
        den = jnp.sum(pp, axis=-1, keepdims=True) + jnp.sum(pn, axis=-1, keepdims=True)
        o = jnp.dot(pp.astype(BF16), vbuf[slot, :, h, :].astype(BF16), preferred_element_type=F32)
        vnew = vnew_ref[0, h]
        for j in range(s_new):
            o = o + pn[:, j:j + 1] * vnew[j:j + 1, :]
        o_ref[0, :, h * ATT_HD:(h + 1) * ATT_HD] = (o / den)[:s_new]


def _dsa_sample(qb, kb, v32, qib, kid, wi, ck, cv, cki, layer, page_table, db, s):
    n_pages = page_table.shape[1]
    past = n_pages * PAGE_SIZE
    topk = min(TOPK_MAX, (past + s) // 4)
    heads = lambda u, n: jnp.pad(u.reshape(db, s, ATT_HEADS, ATT_HD), ((0, 0), (0, n - s), (0, 0), (0, 0))
                                 ).transpose(0, 2, 1, 3)
    q4, knew, vnew = heads(qb, QS), heads(kb, NS), heads(v32, NS)
    a = qib.reshape(db, s, IDX_HEADS, 2, IDX_HD).sum(axis=3)
    a = jnp.pad(a, ((0, 0), (0, QS - s), (0, 0), (0, LANES - IDX_HD)))
    a = a.transpose(0, 2, 1, 3).reshape(db, IDX_HEADS * QS, LANES)
    wcol = jnp.pad(wi.reshape(db, s, IDX_HEADS), ((0, 0), (0, QS - s), (0, 0)))
    wcol = wcol.transpose(0, 2, 1).reshape(db, IDX_HEADS * QS, 1)
    kinew = jnp.pad(kid.reshape(db, s, LANES)[:, :, :IDX_HD], ((0, 0), (0, NS - s), (0, LANES - IDX_HD)))
    per = lambda r, w: pl.BlockSpec((1, r, w), lambda i, pt: (i, 0, 0))
    perh = lambda r: pl.BlockSpec((1, ATT_HEADS, r, ATT_HD), lambda i, pt: (i, 0, 0, 0))
    hbm = pl.BlockSpec(memory_space=pl.ANY)
    grid_spec = pltpu.PrefetchScalarGridSpec(
        num_scalar_prefetch=1,
        grid=(db,),
        in_specs=[perh(QS), perh(NS), perh(NS), per(IDX_HEADS * QS, LANES), per(IDX_HEADS * QS, 1), per(NS, LANES),
                  hbm, hbm, hbm],
        out_specs=per(s, ATT_W),
        scratch_shapes=[pltpu.VMEM((2, past, ATT_HEADS, ATT_HD), F32), pltpu.VMEM((2, past, ATT_HEADS, ATT_HD), F32),
                        pltpu.VMEM((2, past, IDX_HD), F32), pltpu.SemaphoreType.DMA((3, 2))],
    )
    out = pl.pallas_call(
        functools.partial(_dsa_sample_kernel, layer=layer, n_pages=n_pages, s_new=s, topk=topk),
        grid_spec=grid_spec,
        out_shape=jax.ShapeDtypeStruct((db, s, ATT_W), F32),
        compiler_params=_cparams("arbitrary"),
        name="dsa_sample",
    )(page_table, q4, knew, vnew, a, wcol, kinew, ck, cv, cki)
    return out.reshape(db * s, ATT_W)


def _head_ones():
    i = jnp.arange(RW_W) // RW_HD
    return (i[:, None] == i[None, :]).astype(F32)


def _rw_prep_kernel(z_ref, p_ref, mu_ref, w0_ref, a0_ref, kkg_ref, ka_ref, w2_ref, a2_ref, g2_ref, seg_ref,
                    r_ref, lw_ref, k_ref, v_ref, kk_ref, a_ref, g_ref):
    z = z_ref[...]
    xm = z + (p_ref[...] - z) * mu_ref[...]
    r = xm[:, :RW_W]
    k = xm[:, RW_W:2 * RW_W]
    tail = xm[:, 3 * RW_W:]
    dw = jnp.dot(jnp.tanh(tail), w2_ref[...], precision=HI, preferred_element_type=F32)
    da = jnp.dot(tail, a2_ref[...], precision=HI, preferred_element_type=F32)
    lw_ref[...] = -_sigmoid(w0_ref[...] + dw) * math.exp(-0.5)
    a = _sigmoid(a0_ref[...] + da)
    g_ref[...] = jnp.dot(_sigmoid(tail), g2_ref[...], precision=HI, preferred_element_type=F32)
    kk = k * kkg_ref[...]
    ss = jnp.dot(kk * kk, seg_ref[...], precision=HI, preferred_element_type=F32)
    kk_ref[...] = kk / jnp.maximum(jnp.sqrt(ss), 1e-12)
    k_ref[...] = k * (1.0 + (a - 1.0) * ka_ref[...])
    r_ref[...] = r
    v_ref[...] = xm[:, 2 * RW_W:3 * RW_W]
    a_ref[...] = a


def _rw_prep(zr, prev, mu, w0, a0, kkg, ka, w2f, a2f, g2f, seg):
    m = zr.shape[0]
    tm = min(512, m)
    row = lambda w: pl.BlockSpec((tm, w), lambda i: (i, 0))
    const = lambda r, w: pl.BlockSpec((r, w), lambda i: (0, 0))
    return pl.pallas_call(
        _rw_prep_kernel,
        grid=(m // tm,),
        in_specs=[row(RW_IN), row(RW_IN), const(1, RW_IN)] + [const(1, RW_W)] * 4
                 + [const(LANES, RW_W)] * 3 + [const(RW_W, RW_W)],
        out_specs=[row(RW_W)] * 7,
        out_shape=[jax.ShapeDtypeStruct((m, RW_W), F32)] * 7,
        compiler_params=_cparams("parallel"),
        name="rwkv_prep",
    )(zr, prev, mu, w0, a0, kkg, ka, w2f, a2f, g2f, seg)


RW_CHUNK = LANES


def _rw_scan_kernel(r_ref, lw_ref, k_ref, v_ref, kk_ref, a_ref, h0_ref, y_ref, hT_ref, st_ref, *, nchunk):
    c = RW_CHUNK
    npair = RW_HEADS // 2
    tb = pl.program_id(1)

    @pl.when(tb == 0)
    def _():
        st_ref[...] = jnp.zeros_like(st_ref)
        for h in range(RW_HEADS):
            o = (h % 2) * RW_HD
            st_ref[h // 2, o:o + RW_HD, o:o + RW_HD] = h0_ref[0, h]

    ri = lax.broadcasted_iota(jnp.int32, (c, c), 0)
    ci = lax.broadcasted_iota(jnp.int32, (c, c), 1)
    low_incl = ci <= ri
    low_strict = ci < ri
    diag = ci == ri
    tri = jnp.where(low_incl, 1.0, 0.0)
    eye = jnp.where(diag, 1.0, 0.0)
    half0 = ci < RW_HD
    blockdiag = (ri // RW_HD) == (ci // RW_HD)
    nsq = int(math.log2(c))
    nn = (((1,), (0,)), ((), ()))
    nt = (((1,), (1,)), ((), ()))

    def chunk(ic, carry):
        rows = pl.ds(pl.multiple_of(ic * c, c), c)
        for p in range(npair):
            lanes = slice(p * LANES, (p + 1) * LANES)
            lw = lw_ref[rows, lanes]
            r = r_ref[rows, lanes]
            k = k_ref[rows, lanes]
            v = v_ref[rows, lanes]
            kk = kk_ref[rows, lanes]
            a = a_ref[rows, lanes]
            cum = jnp.dot(tri, lw, precision=HI, preferred_element_type=F32)
            tot = cum[c - 1:c, :]
            pinv = jnp.exp(-cum)
            abar = -kk * jnp.exp(cum - lw)
            rbar = r * jnp.exp(cum)
            bt = kk * a * pinv
            kt = k * pinv
            pend = jnp.exp(tot - cum)
            bhat = kk * a * pend
            khat = k * pend
            hst = st_ref[p]
            bt_s, kt_s, v_s, hst_s = _split(bt), _split(kt), _split(v), _split(hst)
            u = None
            y = None
            for hf in range(2):
                msk = half0 if hf == 0 else ~half0
                ah_s = _split(jnp.where(msk, abar, 0.0))
                rh_s = _split(jnp.where(msk, rbar, 0.0))
                lab = jnp.where(low_strict, _dot3(ah_s, bt_s, nt), 0.0)
                lak = jnp.where(low_strict, _dot3(ah_s, kt_s, nt), 0.0)
                mrb = jnp.where(low_incl, _dot3(rh_s, bt_s, nt), 0.0)
                mrk = jnp.where(low_incl, _dot3(rh_s, kt_s, nt), 0.0)
                tinv = eye + lab
                pw = lab
                for _ in range(nsq - 1):
                    pw_s = _split(pw)
                    pw = _dot3(pw_s, pw_s, nn)
                    tinv = _dot3(_split(tinv), _split(eye + pw), nn)
                x = _dot3(_cat([ah_s, _split(lak)], 1), _cat([hst_s, v_s], 0), nn)
                uh = _dot3(_split(tinv), _split(x), nn)
                yh = _dot3(_cat([rh_s, _split(mrb), _split(mrk)], 1), _cat([hst_s, _split(uh), v_s], 0), nn)
                u = uh if hf == 0 else jnp.where(half0, u, uh)
                y = yh if hf == 0 else jnp.where(half0, y, yh)
            y_ref[rows, lanes] = y
            decay = jnp.where(diag, jnp.exp(tot), 0.0)
            lhs_t = jnp.concatenate([bhat, khat, decay], axis=0).T
            upd = _dot3(_split(lhs_t), _cat([_split(u), v_s, hst_s], 0), nn)
            st_ref[p] = jnp.where(blockdiag, upd, 0.0)
        return carry

    lax.fori_loop(0, nchunk, chunk, 0)

    @pl.when(tb == pl.num_programs(1) - 1)
    def _():
        for h in range(RW_HEADS):
            o = (h % 2) * RW_HD
            hT_ref[0, h] = st_ref[h // 2, o:o + RW_HD, o:o + RW_HD]


def _rw_scan(r, lw, k, v, kk, a, s0, b, t):
    tb = min(512, t)
    nt = t // tb
    row = pl.BlockSpec((tb, RW_W), lambda i, j: (i * nt + j, 0))
    st = pl.BlockSpec((1, RW_HEADS, RW_HD, RW_HD), lambda i, j: (i, 0, 0, 0))
    y, h_end = pl.pallas_call(
        functools.partial(_rw_scan_kernel, nchunk=tb // RW_CHUNK),
        grid=(b, nt),
        in_specs=[row] * 6 + [st],
        out_specs=[row, st],
        out_shape=[jax.ShapeDtypeStruct((b * t, RW_W), F32),
                   jax.ShapeDtypeStruct((b, RW_HEADS, RW_HD, RW_HD), F32)],
        scratch_shapes=[pltpu.VMEM((RW_HEADS // 2, LANES, LANES), F32)],
        compiler_params=_cparams("parallel", "arbitrary"),
        name="rwkv_scan",
    )(r, lw, k, v, kk, a, jnp.swapaxes(s0, -1, -2))
    return y, jnp.swapaxes(h_end, -1, -2)


RW_STEP_SEQS = 8


def _rw_step_kernel(r_ref, lw_ref, k_ref, v_ref, kk_ref, a_ref, s0_ref, seg_ref, y_ref, sT_ref, *, steps):
    nb = s0_ref.shape[0]
    rows = nb * RW_HD
    seg = seg_ref[...]
    st = s0_ref[...].reshape(rows, RW_W)
    vi = lax.broadcasted_iota(jnp.int32, (rows, RW_W), 0) % RW_HD
    ki = lax.broadcasted_iota(jnp.int32, (rows, RW_W), 1) % RW_HD
    dmask = vi == ki
    seg_sum = functools.partial(jnp.dot, precision=HI, preferred_element_type=F32)

    def bc(ref, t):
        return jnp.broadcast_to(ref[:, t:t + 1, :], (nb, RW_HD, RW_W)).reshape(rows, RW_W)

    for t in range(steps):
        kk = bc(kk_ref, t)
        sa = seg_sum(st * (-kk), seg)
        vcol = seg_sum(jnp.where(dmask, bc(v_ref, t), 0.0), seg)
        st = st * jnp.exp(bc(lw_ref, t)) + sa * (kk * bc(a_ref, t)) + vcol * bc(k_ref, t)
        ycol = seg_sum(st * bc(r_ref, t), seg)
        y_ref[:, t:t + 1, :] = jnp.sum(jnp.where(dmask, ycol, 0.0).reshape(nb, RW_HD, RW_W), axis=1, keepdims=True)
    sT_ref[...] = st.reshape(nb, RW_HD, RW_W)


def _rw_steps(r, lw, k, v, kk, a, s0, seg, b, t):
    nb = RW_STEP_SEQS
    row = pl.BlockSpec((nb, t, RW_W), lambda i: (i, 0, 0))
    st = pl.BlockSpec((nb, RW_HD, RW_W), lambda i: (i, 0, 0))
    slab = lambda s: s.transpose(0, 2, 1, 3).reshape(b, RW_HD, RW_W)
    y, s_end = pl.pallas_call(
        functools.partial(_rw_step_kernel, steps=t),
        grid=(b // nb,),
        in_specs=[row] * 6 + [st, pl.BlockSpec((RW_W, RW_W), lambda i: (0, 0))],
        out_specs=[row, st],
        out_shape=[jax.ShapeDtypeStruct((b, t, RW_W), F32), jax.ShapeDtypeStruct((b, RW_HD, RW_W), F32)],
        compiler_params=_cparams("parallel"),
        name="rwkv_steps",
    )(*(u.reshape(b, t, RW_W) for u in (r, lw, k, v, kk, a)), slab(s0), seg)
    return y.reshape(b * t, RW_W), s_end.reshape(b, RW_HD, RW_HEADS, RW_HD).transpose(0, 2, 1, 3)


def _rw_post_kernel(y_ref, r_ref, k_ref, v_ref, g_ref, rk_ref, lg_ref, lb_ref, seg_ref, o_ref):
    seg = seg_ref[...]
    y = y_ref[...]
    mean = jnp.dot(y, seg, precision=HI, preferred_element_type=F32) * (1.0 / RW_HD)
    yc = y - mean
    var = jnp.dot(yc * yc, seg, precision=HI, preferred_element_type=F32) * (1.0 / RW_HD)
    yn = yc * lax.rsqrt(var + RW_LN_EPS) * lg_ref[...] + lb_ref[...]
    v = v_ref[...]
    bonus = jnp.dot(r_ref[...] * k_ref[...] * rk_ref[...], seg, precision=HI, preferred_element_type=F32) * v
    o_ref[...] = ((yn + bonus) * g_ref[...]).astype(BF16)


def _rw_post(y, r, k, v, g, rk, lg, lb, seg):
    m = y.shape[0]
    tm = min(512, m)
    row = pl.BlockSpec((tm, RW_W), lambda i: (i, 0))
    vec = pl.BlockSpec((1, RW_W), lambda i: (0, 0))
    return pl.pallas_call(
        _rw_post_kernel,
        grid=(m // tm,),
        in_specs=[row] * 5 + [vec] * 3 + [pl.BlockSpec((RW_W, RW_W), lambda i: (0, 0))],
        out_specs=row,
        out_shape=jax.ShapeDtypeStruct((m, RW_W), BF16),
        compiler_params=_cparams("parallel"),
        name="rwkv_post",
    )(y, r, k, v, g, rk, lg, lb, seg)


def _s5_mats(lam_re, lam_im, log_step, b_re, b_im, c_re, c_im, cs):
    delta = jnp.exp(log_step)[:, None]
    mag = jnp.exp(lam_re * delta)
    ab_re, ab_im = mag * jnp.cos(lam_im * delta), mag * jnp.sin(lam_im * delta)
    den = lam_re * lam_re + lam_im * lam_im
    n_re, n_im = ab_re - 1.0, ab_im
    q_re = (n_re * lam_re + n_im * lam_im) / den
    q_im = (n_im * lam_re - n_re * lam_im) / den
    bb_re = q_re[..., None] * b_re - q_im[..., None] * b_im
    bb_im = q_re[..., None] * b_im + q_im[..., None] * b_re
    pr, pi = [jnp.ones_like(ab_re)], [jnp.zeros_like(ab_im)]
    for _ in range(cs):
        pr, pi = pr + [pr[-1] * ab_re - pi[-1] * ab_im], pi + [pr[-1] * ab_im + pi[-1] * ab_re]
    pr, pi = jnp.stack(pr), jnp.stack(pi)
    wr = pr[:cs, :, :, None] * bb_re[None] - pi[:cs, :, :, None] * bb_im[None]
    wi = pr[:cs, :, :, None] * bb_im[None] + pi[:cs, :, :, None] * bb_re[None]
    kern = (jnp.einsum('gcp,tgpd->tgdc', c_re, wr, precision=HI)
            - jnp.einsum('gcp,tgpd->tgdc', c_im, wi, precision=HI))
    lag = jnp.arange(cs)[None, :] - jnp.arange(cs)[:, None]
    toep = jnp.where((lag >= 0)[:, :, None, None, None], kern[jnp.maximum(lag, 0)], 0.0)
    toep = toep.transpose(2, 0, 3, 1, 4).reshape(S5_GROUPS, cs * S5_GC, cs * S5_GC)
    rev = jnp.arange(cs - 1, -1, -1)
    gmat = jnp.concatenate([wr[rev], wi[rev]], axis=2)
    gmat = gmat.transpose(1, 0, 3, 2).reshape(S5_GROUPS, cs * S5_GC, 2 * S5_P)
    p1r, p1i = pr[1:], pi[1:]
    h_re = c_re[None] * p1r[:, :, None, :] - c_im[None] * p1i[:, :, None, :]
    h_im = -c_re[None] * p1i[:, :, None, :] - c_im[None] * p1r[:, :, None, :]
    hmat = jnp.concatenate([h_re, h_im], axis=3)
    hmat = hmat.transpose(1, 3, 0, 2).reshape(S5_GROUPS, 2 * S5_P, cs * S5_GC)
    lam_a = jnp.concatenate([pr[cs], pr[cs]], axis=-1)[:, None, :]
    lam_b = jnp.concatenate([-pi[cs], pi[cs]], axis=-1)[:, None, :]
    return toep, gmat, hmat, lam_a, lam_b


def _s5_in_kernel(u_ref, gm_ref, o_ref):
    o_ref[0] = jnp.dot(u_ref[0], gm_ref[0], precision=HI, preferred_element_type=F32)


def _s5_carry_kernel(g_ref, x0_ref, la_ref, lb_ref, xs_ref, xT_ref, *, nj):
    gt, bsz = x0_ref.shape[0], x0_ref.shape[1]
    la = jnp.broadcast_to(la_ref[...], (gt, bsz, LANES)).reshape(gt * bsz, LANES)
    lb = jnp.broadcast_to(lb_ref[...], (gt, bsz, LANES)).reshape(gt * bsz, LANES)

    def step(j, x):
        xs_ref[:, j] = x.reshape(gt, bsz, LANES)
        return x * la + pltpu.roll(x, S5_P, 1) * lb + g_ref[:, j].reshape(gt * bsz, LANES)

    x = lax.fori_loop(0, nj, step, x0_ref[...].reshape(gt * bsz, LANES))
    xT_ref[...] = x.reshape(gt, bsz, LANES)


def _s5_out_kernel(u_ref, xs_ref, tp_ref, hm_ref, d_ref, o_ref):
    u = u_ref[0]
    y = (jnp.dot(u, tp_ref[0], precision=HI, preferred_element_type=F32)
         + jnp.dot(xs_ref[0], hm_ref[0], precision=HI, preferred_element_type=F32))
    o_ref[0] = y + d_ref[0] * u


def _s5_scan(ug, x0, mats, dg, nj, bsz):
    toep, gmat, hmat, lam_a, lam_b = mats
    ng, rows, w = ug.shape
    grp = lambda r, c: pl.BlockSpec((1, r, c), lambda g: (g, 0, 0))
    gin = pl.pallas_call(
        _s5_in_kernel,
        grid=(ng,),
        in_specs=[grp(rows, w), grp(w, LANES)],
        out_specs=grp(rows, LANES),
        out_shape=jax.ShapeDtypeStruct((ng, rows, LANES), F32),
        compiler_params=_cparams("parallel"),
        name="s5_chunk_in",
    )(ug, gmat)
    gt = 8
    xs, x_last = pl.pallas_call(
        functools.partial(_s5_carry_kernel, nj=nj),
        grid=(ng // gt,),
        in_specs=[pl.BlockSpec((gt, nj, bsz, LANES), lambda g: (g, 0, 0, 0)),
                  pl.BlockSpec((gt, bsz, LANES), lambda g: (g, 0, 0)),
                  pl.BlockSpec((gt, 1, LANES), lambda g: (g, 0, 0)),
                  pl.BlockSpec((gt, 1, LANES), lambda g: (g, 0, 0))],
        out_specs=[pl.BlockSpec((gt, nj, bsz, LANES), lambda g: (g, 0, 0, 0)),
                   pl.BlockSpec((gt, bsz, LANES), lambda g: (g, 0, 0))],
        out_shape=[jax.ShapeDtypeStruct((ng, nj, bsz, LANES), F32),
                   jax.ShapeDtypeStruct((ng, bsz, LANES), F32)],
        compiler_params=_cparams("parallel"),
        name="s5_carry",
    )(gin.reshape(ng, nj, bsz, LANES), x0, lam_a, lam_b)
    yg = pl.pallas_call(
        _s5_out_kernel,
        grid=(ng,),
        in_specs=[grp(rows, w), grp(rows, LANES), grp(w, w), grp(LANES, w), grp(1, w)],
        out_specs=grp(rows, w),
        out_shape=jax.ShapeDtypeStruct((ng, rows, w), F32),
        compiler_params=_cparams("parallel"),
        name="s5_chunk_out",
    )(ug, xs.reshape(ng, rows, LANES), toep, hmat, dg)
    return yg, x_last


def _s5_glu_kernel(y_ref, w1_ref, w2_ref, o_ref):
    y = y_ref[...]
    y = y * (0.5 * (1.0 + jnp.tanh(math.sqrt(2.0 / math.pi) * (y + 0.044715 * (y * y * y)))))
    yb = y.astype(BF16)
    o_ref[...] = (jnp.dot(yb, w1_ref[...], preferred_element_type=F32)
                  * _sigmoid(jnp.dot(yb, w2_ref[...], preferred_element_type=F32))).astype(BF16)


def _s5_glu(y, w1, w2):
    m = y.shape[0]
    tm = min(512, m)
    row = pl.BlockSpec((tm, S5_W), lambda i: (i, 0))
    mat = pl.BlockSpec((S5_W, S5_W), lambda i: (0, 0))
    return pl.pallas_call(
        _s5_glu_kernel,
        grid=(m // tm,),
        in_specs=[row, mat, mat],
        out_specs=row,
        out_shape=jax.ShapeDtypeStruct((m, S5_W), BF16),
        compiler_params=_cparams("parallel"),
        name="s5_glu",
    )(y, w1, w2)


def _s5(zs, x0_re, x0_im, mats, dskip, glu1, glu2, b, t, cs):
    nj = t // cs
    w = cs * S5_GC
    wp = max(w, LANES)
    lpad = lambda m, axes: jnp.pad(m, [(0, wp - w) if ax in axes else (0, 0) for ax in range(m.ndim)])
    toep, gmat, hmat, lam_a, lam_b = mats
    mats = (lpad(toep, (1, 2)), lpad(gmat, (1,)), lpad(hmat, (2,)), lam_a, lam_b)
    ug = zs.reshape(b, nj, cs, S5_GROUPS, S5_GC).transpose(3, 1, 0, 2, 4).reshape(S5_GROUPS, nj * b, w)
    x0 = jnp.concatenate([x0_re, x0_im], axis=-1).transpose(1, 0, 2)
    dg = jnp.tile(dskip.reshape(S5_GROUPS, 1, S5_GC), (1, 1, cs))
    yg, x_last = _s5_scan(lpad(ug, (2,)), x0, mats, lpad(dg, (2,)), nj, b)
    y = yg[:, :, :w].reshape(S5_GROUPS, nj, b, cs, S5_GC).transpose(2, 1, 3, 0, 4).reshape(b * t, S5_W)
    x_last = x_last.transpose(1, 0, 2)
    return _s5_glu(y, glu1, glu2), x_last[..., :S5_P], x_last[..., S5_P:]


def _mix_out_kernel(x_ref, ya_ref, yr_ref, ys_ref, w_ref, o_ref):
    acc = jnp.dot(ya_ref[...].astype(BF16), w_ref[:ATT_W, :], preferred_element_type=F32)
    acc = acc + jnp.dot(yr_ref[...], w_ref[ATT_W:ATT_W + RW_W, :], preferred_element_type=F32)
    acc = acc + jnp.dot(ys_ref[...], w_ref[ATT_W + RW_W:, :], preferred_element_type=F32)
    o_ref[...] = x_ref[...] + acc


def _mix_out(x, ya, yr, ys, w):
    m, d = x.shape
    tm = min(512, m)
    row = lambda c: pl.BlockSpec((tm, c), lambda i: (i, 0))
    return pl.pallas_call(
        _mix_out_kernel,
        grid=(m // tm,),
        in_specs=[row(d), row(ATT_W), row(RW_W), row(S5_W), pl.BlockSpec(w.shape, lambda i: (0, 0))],
        out_specs=row(d),
        out_shape=jax.ShapeDtypeStruct((m, d), F32),
        compiler_params=_cparams("parallel"),
        name="mix_out",
    )(x, ya, yr, ys, w)


def _layer_weights(p, l):
    ffpad = lambda w: jnp.pad(w.astype(BF16), ((0, 0), (0, D_FF_PAD - D_FF)))
    w_in = p['w_in'][l]
    w = {}
    for n in ('ffn1', 'ffn2'):
        w[n] = (p[n + '_norm'][l][None], ffpad(p[n + '_wg'][l]), ffpad(p[n + '_wu'][l]),
                jnp.pad(p[n + '_wd'][l].astype(BF16), ((0, D_FF_PAD - D_FF), (0, 0))))
    w['mix_norm'] = p['mix_norm'][l][None]
    w['w_att'] = jnp.pad(w_in[:, :ATT_IN].astype(BF16), ((0, 0), (0, ATT_Z - ATT_IN)))
    w['w_rw'] = w_in[:, ATT_IN:ATT_IN + RW_IN].astype(BF16)
    w['w_s5'] = w_in[:, ATT_IN + RW_IN:].astype(BF16)
    w['q_norm'] = p['q_norm'][l][None]
    w['k_norm'] = p['k_norm'][l][None]
    tailpad = lambda m, off: jnp.pad(m, ((off, LANES - off - m.shape[0]), (0, 0)))
    w['rw_prep'] = (p['rw_mu'][l][None], p['rw_w0'][l][None], p['rw_a0'][l][None], p['rw_kk'][l][None],
                    p['rw_ka'][l][None], tailpad(p['rw_w2'][l], 0), tailpad(p['rw_a2'][l], RW_DECAY_R),
                    tailpad(p['rw_g2'][l], RW_DECAY_R + RW_A_R), _head_ones())
    w['rw_post'] = (p['rw_rk'][l].reshape(1, RW_W), p['rw_lnx_g'][l][None], p['rw_lnx_b'][l][None], _head_ones())
    w['s5_params'] = (p['s5_lam_re'][l], p['s5_lam_im'][l], p['s5_log_step'][l], p['s5_b_re'][l], p['s5_b_im'][l],
                      p['s5_c_re'][l], p['s5_c_im'][l])
    w['s5_d'] = p['s5_d'][l]
    w['s5_glu'] = (p['s5_glu1'][l].astype(BF16), p['s5_glu2'][l].astype(BF16))
    w['w_out'] = p['w_out'][l].astype(BF16)
    return w


def _layer(x, w, tabs, b, t, attn_fn, shift0, rw0, s5re0, s5im0, s5_chunk):
    m = b * t
    x = _ffn(x, *w['ffn1'])
    za = _nmm(x, w['mix_norm'], w['w_att'], ATT_Z // 3)
    zr = _nmm(x, w['mix_norm'], w['w_rw'], RW_IN)
    zs = _nmm(x, w['mix_norm'], w['w_s5'], S5_W)
    qb, k32, kb, vb, qib, ki32, kid, wi = _attn_prep(za, tabs, w['q_norm'], w['k_norm'])
    v32 = za[:, 2 * ATT_W:3 * ATT_W]
    ya = attn_fn(qb, kb, vb, v32, qib, kid, wi)
    zr3 = zr.reshape(b, t, RW_IN)
    prev = jnp.concatenate([shift0[:, None, :], zr3[:, :-1]], axis=1).reshape(m, RW_IN)
    r, lw, k2, v, kk, a, g = _rw_prep(zr, prev, *w['rw_prep'])
    if t % RW_CHUNK == 0:
        y, rw_new = _rw_scan(r, lw, k2, v, kk, a, rw0, b, t)
    else:
        y, rw_new = _rw_steps(r, lw, k2, v, kk, a, rw0, w['rw_post'][3], b, t)
    yr = _rw_post(y, r, k2, v, g, *w['rw_post'])
    mats = _s5_mats(*w['s5_params'], s5_chunk)
    ys, s5re, s5im = _s5(zs, s5re0, s5im0, mats, w['s5_d'], *w['s5_glu'], b, t, s5_chunk)
    x = _mix_out(x, ya, yr, ys, w['w_out'])
    x = _ffn(x, *w['ffn2'])
    state = (k32.reshape(b, t, ATT_HEADS, ATT_HD), v32.reshape(b, t, ATT_HEADS, ATT_HD),
             ki32.reshape(b, t, IDX_HD), rw_new, zr3[:, -1], s5re, s5im)
    return x, state


def kernel(x_prompt, x_sample, cache_k, cache_v, cache_kidx, state_rwkv, state_shift, state_s5_re, state_s5_im, page_table, ffn1_norm, ffn1_wg, ffn1_wu, ffn1_wd, mix_norm, w_in, q_norm, k_norm, rw_mu, rw_w0, rw_w2, rw_a0, rw_a2, rw_g2, rw_kk, rw_ka, rw_rk, rw_lnx_g, rw_lnx_b, s5_lam_re, s5_lam_im, s5_log_step, s5_b_re, s5_b_im, s5_c_re, s5_c_im, s5_d, s5_glu1, s5_glu2, w_out, ffn2_norm, ffn2_wg, ffn2_wu, ffn2_wd):
    p = dict(ffn1_norm=ffn1_norm, ffn1_wg=ffn1_wg, ffn1_wu=ffn1_wu, ffn1_wd=ffn1_wd, mix_norm=mix_norm, w_in=w_in,
             q_norm=q_norm, k_norm=k_norm, rw_mu=rw_mu, rw_w0=rw_w0, rw_w2=rw_w2, rw_a0=rw_a0, rw_a2=rw_a2,
             rw_g2=rw_g2, rw_kk=rw_kk, rw_ka=rw_ka, rw_rk=rw_rk, rw_lnx_g=rw_lnx_g, rw_lnx_b=rw_lnx_b,
             s5_lam_re=s5_lam_re, s5_lam_im=s5_lam_im, s5_log_step=s5_log_step, s5_b_re=s5_b_re, s5_b_im=s5_b_im,
             s5_c_re=s5_c_re, s5_c_im=s5_c_im, s5_d=s5_d, s5_glu1=s5_glu1, s5_glu2=s5_glu2, w_out=w_out,
             ffn2_norm=ffn2_norm, ffn2_wg=ffn2_wg, ffn2_wu=ffn2_wu, ffn2_wd=ffn2_wd)
    depth = w_in.shape[0]
    b, t, d = x_prompt.shape
    db, s, _ = x_sample.shape
    past = page_table.shape[1] * PAGE_SIZE
    weights = [_layer_weights(p, l) for l in range(depth)]

    tabs_p = _rope_tables(jnp.arange(t))
    attn_p = lambda qb, kb, vb, v32, qib, kid, wi: _dsa_prompt(qb, kb, vb, qib, kid, wi, b, t)
    y = x_prompt.reshape(b * t, d)
    st_p = []
    for l in range(depth):
        y, st = _layer(y, weights[l], tabs_p, b, t, attn_p, jnp.zeros((b, RW_IN), F32),
                       jnp.zeros((b, RW_HEADS, RW_HD, RW_HD), F32), jnp.zeros((b, S5_GROUPS, S5_P), F32),
                       jnp.zeros((b, S5_GROUPS, S5_P), F32), s5_chunk=16)
        st_p.append(st)
    y_prompt = y.reshape(b, t, d)

    ms = db * s
    tm_s = min(512, ms)
    tabs_s = tuple(jnp.tile(tb, (tm_s // s, 1)) for tb in _rope_tables(past + jnp.arange(s)))
    y = x_sample.reshape(ms, d)
    st_s = []
    for l in range(depth):
        attn_s = functools.partial(
            lambda qb, kb, vb, v32, qib, kid, wi, l: _dsa_sample(qb, kb, v32, qib, kid, wi, cache_k, cache_v,
                                                                  cache_kidx, l, page_table, db, s), l=l)
        y, st = _layer(y, weights[l], tabs_s, db, s, attn_s, state_shift[l], state_rwkv[l], state_s5_re[l],
                       state_s5_im[l], s5_chunk=s)
        st_s.append(st)
    y_sample = y.reshape(db, s, d)
    outs_p = [jnp.stack(u) for u in zip(*st_p)]
    outs_s = [jnp.stack(u) for u in zip(*st_s)]
    return (y_prompt, y_sample, *outs_p, *outs_s)
```

```python
import functools
import math

import jax
import jax.numpy as jnp
from jax import lax
from jax.experimental import pallas as pl
from jax.experimental.pallas import tpu as pltpu

F32 = jnp.float32
BF16 = jnp.bfloat16
HI = lax.Precision.HIGHEST

D_MODEL = 2048
PAGE_SIZE = 128
ATT_HD = 128
ATT_W = 1024
ATT_HEADS = 8
IDX_HEADS = 16
IDX_HD = 64
IDX_W = IDX_HEADS * IDX_HD
IDX_WX = IDX_HEADS * 128
TOPK_MAX = 256
ROPE_THETA = 10000.0
RW_HD = 64
RW_W = 512
RW_HEADS = 8
RW_DECAY_R = 32
RW_A_R = 32
RW_G_R = 64
RW_IN = 3 * RW_W + RW_DECAY_R + RW_A_R + RW_G_R
S5_W = 512
S5_GC = 16
S5_GROUPS = 32
S5_P = 64
ATT_IN = 3 * ATT_W + IDX_W + IDX_HD + IDX_HEADS
D_FF = 5504
NORM_EPS = 1e-6
RW_LN_EPS = 64e-5

LANES = 128
VMEM_LIMIT = 52 * 1024 * 1024
FF_TILE = 512
D_FF_PAD = ((D_FF + FF_TILE - 1) // FF_TILE) * FF_TILE
ATT_Z = 4 * ATT_W + LANES
INT_MIN = -2147483648
NEG_INF = float("-inf")


def _cparams(*sem):
    return pltpu.CompilerParams(dimension_semantics=sem, vmem_limit_bytes=VMEM_LIMIT)


def _nt(a, b, precision=None):
    return lax.dot_general(a, b, (((1,), (1,)), ((), ())), precision=precision,
                           preferred_element_type=F32)


def _tn(a, b, precision=None):
    return lax.dot_general(a, b, (((0,), (0,)), ((), ())), precision=precision,
                           preferred_element_type=F32)


def _split(a):
    hi = a.astype(BF16)
    return hi, (a - hi.astype(F32)).astype(BF16)


def _cat(parts, axis):
    return tuple(jnp.concatenate([p[i] for p in parts], axis=axis) for i in range(2))


def _dot3(a, b, dims):
    dg = lambda x, y: lax.dot_general(x, y, dims, preferred_element_type=F32)
    return dg(a[0], b[0]) + dg(a[0], b[1]) + dg(a[1], b[0])


def _sigmoid(x):
    return 1.0 / (1.0 + jnp.exp(-x))


def _rms(x, g):
    ms = jnp.mean(x * x, axis=-1, keepdims=True)
    return x * lax.rsqrt(ms + NORM_EPS) * g


def _ffn_kernel(x_ref, g_ref, wg_ref, wu_ref, wd_ref, o_ref, xn_ref):
    j = pl.program_id(1)

    @pl.when(j == 0)
    def _():
        xn_ref[...] = _rms(x_ref[...], g_ref[...]).astype(BF16)
        o_ref[...] = jnp.zeros_like(o_ref)

    xn = xn_ref[...]
    hg = jnp.dot(xn, wg_ref[...], preferred_element_type=F32)
    hu = jnp.dot(xn, wu_ref[...], preferred_element_type=F32)
    h = (hg * _sigmoid(hg) * hu).astype(BF16)
    o_ref[...] += jnp.dot(h, wd_ref[...], preferred_element_type=F32)

    @pl.when(j == pl.num_programs(1) - 1)
    def _():
        o_ref[...] = x_ref[...] + 0.5 * o_ref[...]


def _ffn(x, g, wg, wu, wd):
    m, d = x.shape
    tm = min(512, m)
    nf = wg.shape[1] // FF_TILE
    return pl.pallas_call(
        _ffn_kernel,
        grid=(m // tm, nf),
        in_specs=[
            pl.BlockSpec((tm, d), lambda i, j: (i, 0)),
            pl.BlockSpec((1, d), lambda i, j: (0, 0)),
            pl.BlockSpec((d, FF_TILE), lambda i, j: (0, j)),
            pl.BlockSpec((d, FF_TILE), lambda i, j: (0, j)),
            pl.BlockSpec((FF_TILE, d), lambda i, j: (j, 0)),
        ],
        out_specs=pl.BlockSpec((tm, d), lambda i, j: (i, 0)),
        out_shape=jax.ShapeDtypeStruct((m, d), F32),
        scratch_shapes=[pltpu.VMEM((tm, d), BF16)],
        compiler_params=_cparams("parallel", "arbitrary"),
        name="ffn",
    )(x, g, wg, wu, wd)


def _nmm_kernel(x_ref, g_ref, w_ref, o_ref, xn_ref):
    @pl.when(pl.program_id(1) == 0)
    def _():
        xn_ref[...] = _rms(x_ref[...], g_ref[...]).astype(BF16)

    o_ref[...] = jnp.dot(xn_ref[...], w_ref[...], preferred_element_type=F32)


def _nmm(x, g, w, tn):
    m, d = x.shape
    n = w.shape[1]
    tm = min(512, m)
    return pl.pallas_call(
        _nmm_kernel,
        grid=(m // tm, n // tn),
        in_specs=[
            pl.BlockSpec((tm, d), lambda i, j: (i, 0)),
            pl.BlockSpec((1, d), lambda i, j: (0, 0)),
            pl.BlockSpec((d, tn), lambda i, j: (0, j)),
        ],
        out_specs=pl.BlockSpec((tm, tn), lambda i, j: (i, j)),
        out_shape=jax.ShapeDtypeStruct((m, n), F32),
        scratch_shapes=[pltpu.VMEM((tm, d), BF16)],
        compiler_params=_cparams("parallel", "arbitrary"),
        name="norm_proj",
    )(x, g, w)


def _rope_tables(pos):
    def tab(half, reps):
        inv = ROPE_THETA ** (-jnp.arange(half, dtype=F32) / half)
        ang = pos.astype(F32)[:, None] * inv[None, :]
        c, s = jnp.cos(ang), jnp.sin(ang)
        return jnp.tile(jnp.concatenate([c, c], -1), (1, reps)), jnp.tile(jnp.concatenate([-s, s], -1), (1, reps))
    c128, s128 = tab(ATT_HD // 2, 1)
    c64, s64 = tab(IDX_HD // 2, 2)
    return c128, s128, c64, s64


def _rope128(x, c, s):
    return x * c + pltpu.roll(x, ATT_HD // 2, 1) * s


def _rope64(x, c, s):
    lane = lax.broadcasted_iota(jnp.int32, x.shape, 1)
    first = (lane % IDX_HD) < IDX_HD // 2
    partner = jnp.where(first, pltpu.roll(x, LANES - IDX_HD // 2, 1), pltpu.roll(x, IDX_HD // 2, 1))
    return x * c + partner * s


def _prep_kernel(q_ref, k_ref, v_ref, qi_ref, kw_ref, c128_ref, s128_ref, c64_ref, s64_ref, qn_ref, kn_ref,
                 qo_ref, ko_ref, kb_ref, vb_ref, qio_ref, kio_ref, kid_ref, wio_ref):
    c128, s128 = c128_ref[...], s128_ref[...]
    c64, s64 = c64_ref[...], s64_ref[...]
    for h in range(ATT_HEADS):
        sl = slice(h * ATT_HD, (h + 1) * ATT_HD)
        q = _rope128(_rms(q_ref[:, sl], qn_ref[...]), c128, s128)
        k = _rope128(_rms(k_ref[:, sl], kn_ref[...]), c128, s128)
        qo_ref[:, sl] = q.astype(BF16)
        ko_ref[:, sl] = k
        kb_ref[:, sl] = k.astype(BF16)
    vb_ref[...] = v_ref[...].astype(BF16)
    lane = lax.broadcasted_iota(jnp.int32, c64.shape, 1)
    for p in range(IDX_W // LANES):
        qi = _rope64(qi_ref[:, p * LANES:(p + 1) * LANES], c64, s64)
        qio_ref[:, 2 * p * LANES:(2 * p + 1) * LANES] = jnp.where(lane < IDX_HD, qi, 0.0).astype(BF16)
        qio_ref[:, (2 * p + 1) * LANES:(2 * p + 2) * LANES] = jnp.where(lane < IDX_HD, 0.0, qi).astype(BF16)
    kw = kw_ref[...]
    ki = _rope64(kw, c64, s64)
    kio_ref[...] = ki[:, :IDX_HD]
    kid_ref[...] = jnp.where(lane < IDX_HD, ki, pltpu.roll(ki, IDX_HD, 1)).astype(BF16)
    wio_ref[...] = kw[:, IDX_HD:IDX_HD + IDX_HEADS] * (IDX_W ** -0.5)


def _attn_prep(za, tabs, q_norm, k_norm):
    m = za.shape[0]
    tm = min(512, m)
    nt = tabs[0].shape[0] // tm
    wide = lambda c: pl.BlockSpec((tm, ATT_W), lambda i: (i, c))
    tab = pl.BlockSpec((tm, LANES), lambda i: (i % nt, 0))
    vec = pl.BlockSpec((1, ATT_HD), lambda i: (0, 0))
    row = lambda w: pl.BlockSpec((tm, w), lambda i: (i, 0))
    sds = lambda w, dt: jax.ShapeDtypeStruct((m, w), dt)
    return pl.pallas_call(
        _prep_kernel,
        grid=(m // tm,),
        in_specs=[wide(0), wide(1), wide(2), wide(3),
                  pl.BlockSpec((tm, LANES), lambda i: (i, 4 * ATT_W // LANES)),
                  tab, tab, tab, tab, vec, vec],
        out_specs=[row(ATT_W), row(ATT_W), row(ATT_W), row(ATT_W), row(IDX_WX), row(IDX_HD), row(LANES),
                   row(IDX_HEADS)],
        out_shape=[sds(ATT_W, BF16), sds(ATT_W, F32), sds(ATT_W, BF16), sds(ATT_W, BF16), sds(IDX_WX, BF16),
                   sds(IDX_HD, F32), sds(LANES, BF16), sds(IDX_HEADS, F32)],
        compiler_params=_cparams("parallel"),
        name="attn_prep",
    )(za, za, za, za, za, *tabs, q_norm, k_norm)


def _order_key(x):
    bits = lax.bitcast_convert_type(x, jnp.int32)
    return bits ^ ((bits >> 31) & jnp.int32(0x7FFFFFFF))


def _kth_largest(count_ge, rows, k):
    def body(i, cur):
        cand = cur + (jnp.int32(1) << (31 - i))
        return jnp.where(count_ge(cand) >= k, cand, cur)
    return lax.fori_loop(0, 32, body, jnp.full((rows, 1), INT_MIN, jnp.int32))


def _dsa_prompt_kernel(q_ref, k_ref, v_ref, qi_ref, kid_ref, wi_ref, o_ref, key_ref, bias_ref, *, tq, t0, topk):
    tk = t0 + tq
    kid = kid_ref[0]
    wi = wi_ref[...]
    acc = jnp.zeros((tq, tk), F32)
    for h in range(IDX_HEADS):
        acc = acc + jnp.maximum(_nt(qi_ref[:, h * LANES:(h + 1) * LANES], kid), 0.0) * wi[:, h:h + 1]
    col = lax.broadcasted_iota(jnp.int32, (tq, tk), 1)
    row = t0 + lax.broadcasted_iota(jnp.int32, (tq, tk), 0)
    causal = col <= row
    key_ref[...] = _order_key(jnp.where(causal, acc, NEG_INF))

    def count_ge(cand):
        return jnp.sum(jnp.where(key_ref[...] >= cand, 1.0, 0.0), axis=-1, keepdims=True)

    thr = _kth_largest(count_ge, tq, topk)
    bias_ref[...] = jnp.where((key_ref[...] >= thr) & causal, 0.0, NEG_INF)
    scale = ATT_HD ** -0.5
    for h in range(ATT_HEADS):
        sl = slice(h * ATT_HD, (h + 1) * ATT_HD)
        s = _nt(q_ref[:, sl], k_ref[0, :, sl]) * scale + bias_ref[...]
        mx = jnp.max(s, axis=-1, keepdims=True)
        pr = jnp.exp(s - mx)
        den = jnp.sum(pr, axis=-1, keepdims=True)
        o_ref[:, sl] = jnp.dot(pr.astype(BF16), v_ref[0, :, sl], preferred_element_type=F32) / den


def _dsa_prompt(qb, kb, vb, qib, kid, wi, b, t):
    topk = min(TOPK_MAX, t // 4)
    tq = min(256, t)
    nq = t // tq
    k3, v3, kid3 = kb.reshape(b, t, ATT_W), vb.reshape(b, t, ATT_W), kid.reshape(b, t, LANES)
    outs = []
    for j in range(nq):
        tk = (j + 1) * tq
        blk = lambda w, j=j: pl.BlockSpec((tq, w), lambda i: (i * nq + j, 0))
        keys = lambda w, tk=tk: pl.BlockSpec((1, tk, w), lambda i: (i, 0, 0))
        outs.append(pl.pallas_call(
            functools.partial(_dsa_prompt_kernel, tq=tq, t0=j * tq, topk=topk),
            grid=(b,),
            in_specs=[blk(ATT_W), keys(ATT_W), keys(ATT_W), blk(IDX_WX), keys(LANES), blk(IDX_HEADS)],
            out_specs=pl.BlockSpec((tq, ATT_W), lambda i: (i, 0)),
            out_shape=jax.ShapeDtypeStruct((b * tq, ATT_W), F32),
            scratch_shapes=[pltpu.VMEM((tq, tk), jnp.int32), pltpu.VMEM((tq, tk), F32)],
            compiler_params=_cparams("parallel"),
            name="dsa_prompt",
        )(qb, k3, v3, qib, kid3, wi))
    return jnp.stack([o.reshape(b, tq, ATT_W) for o in outs], axis=1).reshape(b * t, ATT_W)


QS = 8
NS = 16


def _dsa_sample_kernel(pt_ref, q_ref, knew_ref, vnew_ref, a_ref, wcol_ref, kinew_ref, ck_hbm, cv_hbm, cki_hbm,
                       o_ref, kbuf, vbuf, kibuf, sem, *, layer, n_pages, s_new, topk):
    b = pl.program_id(0)
    nb = pl.num_programs(0)
    slot = b % 2
    past = n_pages * PAGE_SIZE

    def copies(bb, sl):
        out = []
        for p in range(n_pages):
            page = pt_ref[bb, p]
            dst = pl.ds(p * PAGE_SIZE, PAGE_SIZE)
            for h in range(ATT_HEADS):
                win = pl.ds(h * ATT_HD, ATT_HD)
                out.append(pltpu.make_async_copy(ck_hbm.at[layer, page, :, h], kbuf.at[sl, dst, win], sem.at[0, sl]))
                out.append(pltpu.make_async_copy(cv_hbm.at[layer, page, :, h], vbuf.at[sl, dst, win], sem.at[1, sl]))
            out.append(pltpu.make_async_copy(cki_hbm.at[layer, page], kibuf.at[sl, dst], sem.at[2, sl]))
        return out

    @pl.when(b == 0)
    def _():
        for c in copies(0, 0):
            c.start()

    @pl.when(b + 1 < nb)
    def _():
        for c in copies(b + 1, 1 - slot):
            c.start()

    for c in copies(b, slot):
        c.wait()

    a = a_ref[0]
    wcol = wcol_ref[0]
    ki = kibuf[slot].astype(BF16)
    ki = jnp.concatenate([ki, jnp.zeros_like(ki)], axis=1)
    ip = (jnp.maximum(_nt(a, ki), 0.0) * wcol).reshape(IDX_HEADS, QS, past).sum(axis=0)
    inw = (jnp.maximum(_nt(a, kinew_ref[0]), 0.0) * wcol).reshape(IDX_HEADS, QS, NS).sum(axis=0)
    qrow = lax.broadcasted_iota(jnp.int32, (QS, NS), 0)
    jcol = lax.broadcasted_iota(jnp.int32, (QS, NS), 1)
    new_ok = (jcol <= qrow) & (jcol < s_new)
    key_p = _order_key(ip)
    key_n = _order_key(jnp.where(new_ok, inw, NEG_INF))

    def count_ge(cand):
        return (jnp.sum(jnp.where(key_p >= cand, 1.0, 0.0), axis=-1, keepdims=True)
                + jnp.sum(jnp.where(key_n >= cand, 1.0, 0.0), axis=-1, keepdims=True))

    thr = _kth_largest(count_ge, QS, topk)
    bias_p = jnp.where(key_p >= thr, 0.0, NEG_INF)
    bias_n = jnp.where((key_n >= thr) & new_ok, 0.0, NEG_INF)

    scale = ATT_HD ** -0.5
    qbd = q_ref[0]
    sp = (_nt(qbd, kbuf[slot].astype(BF16)) * scale).reshape(ATT_HEADS, QS, past) + bias_p[None]
    sn = (_nt(qbd, knew_ref[0]) * scale).reshape(ATT_HEADS, QS, NS) + bias_n[None]
    mx = jnp.maximum(jnp.max(sp, axis=-1, keepdims=True), jnp.max(sn, axis=-1, keepdims=True))
    pp = jnp.exp(sp - mx)
    pn = jnp.exp(sn - mx)
    den = jnp.sum(pp, axis=-1, keepdims=True) + jnp.sum(pn, axis=-1, keepdims=True)
    o = jnp.dot(pp.reshape(ATT_HEADS * QS, past).astype(BF16), vbuf[slot].astype(BF16), preferred_element_type=F32)
    pn2 = pn.reshape(ATT_HEADS * QS, NS)
    vnew = vnew_ref[0]
    for j in range(s_new):
        o = o + pn2[:, j:j + 1] * vnew[j:j + 1, :]
    o = o / den.reshape(ATT_HEADS * QS, 1)
    for h in range(ATT_HEADS):
        o_ref[0, :, h * ATT_HD:(h + 1) * ATT_HD] = o[h * QS:h * QS + s_new, h * ATT_HD:(h + 1) * ATT_HD]


def _dsa_sample(qb, kb, v32, qib, kid, wi, ck, cv, cki, layer, page_table, db, s):
    n_pages = page_table.shape[1]
    past = n_pages * PAGE_SIZE
    topk = min(TOPK_MAX, (past + s) // 4)
    q4 = jnp.pad(qb.reshape(db, s, ATT_HEADS, ATT_HD), ((0, 0), (0, QS - s), (0, 0), (0, 0)))
    eye = jnp.eye(ATT_HEADS, dtype=BF16)
    qbd = (q4.transpose(0, 2, 1, 3)[:, :, :, None, :] * eye[None, :, None, :, None]).reshape(db, ATT_HEADS * QS, ATT_W)
    knew = jnp.pad(kb.reshape(db, s, ATT_W), ((0, 0), (0, NS - s), (0, 0)))
    vnew = jnp.pad(v32.reshape(db, s, ATT_W), ((0, 0), (0, NS - s), (0, 0)))
    a = qib.reshape(db, s, IDX_HEADS, 2, IDX_HD).sum(axis=3)
    a = jnp.pad(a, ((0, 0), (0, QS - s), (0, 0), (0, LANES - IDX_HD)))
    a = a.transpose(0, 2, 1, 3).reshape(db, IDX_HEADS * QS, LANES)
    wcol = jnp.pad(wi.reshape(db, s, IDX_HEADS), ((0, 0), (0, QS - s), (0, 0)))
    wcol = wcol.transpose(0, 2, 1).reshape(db, IDX_HEADS * QS, 1)
    kinew = jnp.pad(kid.reshape(db, s, LANES)[:, :, :IDX_HD], ((0, 0), (0, NS - s), (0, LANES - IDX_HD)))
    per = lambda r, w: pl.BlockSpec((1, r, w), lambda i, pt: (i, 0, 0))
    hbm = pl.BlockSpec(memory_space=pl.ANY)
    grid_spec = pltpu.PrefetchScalarGridSpec(
        num_scalar_prefetch=1,
        grid=(db,),
        in_specs=[per(ATT_HEADS * QS, ATT_W), per(NS, ATT_W), per(NS, ATT_W), per(IDX_HEADS * QS, LANES),
                  per(IDX_HEADS * QS, 1), per(NS, LANES), hbm, hbm, hbm],
        out_specs=per(s, ATT_W),
        scratch_shapes=[pltpu.VMEM((2, past, ATT_W), F32), pltpu.VMEM((2, past, ATT_W), F32),
                        pltpu.VMEM((2, past, IDX_HD), F32), pltpu.SemaphoreType.DMA((3, 2))],
    )
    out = pl.pallas_call(
        functools.partial(_dsa_sample_kernel, layer=layer, n_pages=n_pages, s_new=s, topk=topk),
        grid_spec=grid_spec,
        out_shape=jax.ShapeDtypeStruct((db, s, ATT_W), F32),
        compiler_params=_cparams("arbitrary"),
        name="dsa_sample",
    )(page_table, qbd, knew, vnew, a, wcol, kinew, ck, cv, cki)
    return out.reshape(db * s, ATT_W)


def _head_ones():
    i = jnp.arange(RW_W) // RW_HD
    return (i[:, None] == i[None, :]).astype(F32)


def _rw_prep_kernel(z_ref, p_ref, mu_ref, w0_ref, a0_ref, kkg_ref, ka_ref, w2_ref, a2_ref, g2_ref, seg_ref,
                    r_ref, lw_ref, k_ref, v_ref, kk_ref, a_ref, g_ref):
    z = z_ref[...]
    xm = z + (p_ref[...] - z) * mu_ref[...]
    r = xm[:, :RW_W]
    k = xm[:, RW_W:2 * RW_W]
    tail = xm[:, 3 * RW_W:]
    dw = jnp.dot(jnp.tanh(tail), w2_ref[...], precision=HI, preferred_element_type=F32)
    da = jnp.dot(tail, a2_ref[...], precision=HI, preferred_element_type=F32)
    lw_ref[...] = -_sigmoid(w0_ref[...] + dw) * math.exp(-0.5)
    a = _sigmoid(a0_ref[...] + da)
    g_ref[...] = jnp.dot(_sigmoid(tail), g2_ref[...], precision=HI, preferred_element_type=F32)
    kk = k * kkg_ref[...]
    ss = jnp.dot(kk * kk, seg_ref[...], precision=HI, preferred_element_type=F32)
    kk_ref[...] = kk / jnp.maximum(jnp.sqrt(ss), 1e-12)
    k_ref[...] = k * (1.0 + (a - 1.0) * ka_ref[...])
    r_ref[...] = r
    v_ref[...] = xm[:, 2 * RW_W:3 * RW_W]
    a_ref[...] = a


def _rw_prep(zr, prev, mu, w0, a0, kkg, ka, w2f, a2f, g2f, seg):
    m = zr.shape[0]
    tm = min(512, m)
    row = lambda w: pl.BlockSpec((tm, w), lambda i: (i, 0))
    const = lambda r, w: pl.BlockSpec((r, w), lambda i: (0, 0))
    return pl.pallas_call(
        _rw_prep_kernel,
        grid=(m // tm,),
        in_specs=[row(RW_IN), row(RW_IN), const(1, RW_IN)] + [const(1, RW_W)] * 4
                 + [const(LANES, RW_W)] * 3 + [const(RW_W, RW_W)],
        out_specs=[row(RW_W)] * 7,
        out_shape=[jax.ShapeDtypeStruct((m, RW_W), F32)] * 7,
        compiler_params=_cparams("parallel"),
        name="rwkv_prep",
    )(zr, prev, mu, w0, a0, kkg, ka, w2f, a2f, g2f, seg)


RW_CHUNK = LANES


def _rw_scan_kernel(r_ref, lw_ref, k_ref, v_ref, kk_ref, a_ref, h0_ref, y_ref, hT_ref, st_ref, *, nchunk):
    c = RW_CHUNK
    npair = RW_HEADS // 2
    tb = pl.program_id(1)

    @pl.when(tb == 0)
    def _():
        st_ref[...] = jnp.zeros_like(st_ref)
        for h in range(RW_HEADS):
            o = (h % 2) * RW_HD
            st_ref[h // 2, o:o + RW_HD, o:o + RW_HD] = h0_ref[0, h]

    ri = lax.broadcasted_iota(jnp.int32, (c, c), 0)
    ci = lax.broadcasted_iota(jnp.int32, (c, c), 1)
    low_incl = ci <= ri
    low_strict = ci < ri
    diag = ci == ri
    tri = jnp.where(low_incl, 1.0, 0.0)
    eye = jnp.where(diag, 1.0, 0.0)
    half0 = ci < RW_HD
    blockdiag = (ri // RW_HD) == (ci // RW_HD)
    nsq = int(math.log2(c))
    nn = (((1,), (0,)), ((), ()))
    nt = (((1,), (1,)), ((), ()))

    def chunk(ic, carry):
        rows = pl.ds(pl.multiple_of(ic * c, c), c)
        pairs = []
        for p in range(npair):
            lanes = slice(p * LANES, (p + 1) * LANES)
            lw = lw_ref[rows, lanes]
            r = r_ref[rows, lanes]
            k = k_ref[rows, lanes]
            v = v_ref[rows, lanes]
            kk = kk_ref[rows, lanes]
            a = a_ref[rows, lanes]
            cum = jnp.dot(tri, lw, precision=HI, preferred_element_type=F32)
            tot = cum[c - 1:c, :]
            pinv = jnp.exp(-cum)
            abar = -kk * jnp.exp(cum - lw)
            rbar = r * jnp.exp(cum)
            pend = jnp.exp(tot - cum)
            hst = st_ref[p]
            decay = jnp.where(diag, jnp.exp(tot), 0.0)
            pairs.append(dict(lanes=lanes, abar=abar, rbar=rbar, bt_s=_split(kk * a * pinv), kt_s=_split(k * pinv),
                              v_s=_split(v), hst_s=_split(hst),
                              lhs_t=jnp.concatenate([kk * a * pend, k * pend, decay], axis=0).T))
        chains = []
        for q in pairs:
            for hf in range(2):
                msk = half0 if hf == 0 else ~half0
                ah_s = _split(jnp.where(msk, q['abar'], 0.0))
                rh_s = _split(jnp.where(msk, q['rbar'], 0.0))
                lab = jnp.where(low_strict, _dot3(ah_s, q['bt_s'], nt), 0.0)
                chains.append(dict(q=q, ah_s=ah_s, rh_s=rh_s, tinv=eye + lab, pw=lab,
                                   lak=jnp.where(low_strict, _dot3(ah_s, q['kt_s'], nt), 0.0),
                                   mrb=jnp.where(low_incl, _dot3(rh_s, q['bt_s'], nt), 0.0),
                                   mrk=jnp.where(low_incl, _dot3(rh_s, q['kt_s'], nt), 0.0)))
        for _ in range(nsq - 1):
            for ch in chains:
                pw_s = _split(ch['pw'])
                ch['pw'] = _dot3(pw_s, pw_s, nn)
            for ch in chains:
                ch['tinv'] = _dot3(_split(ch['tinv']), _split(eye + ch['pw']), nn)
        for ch in chains:
            q = ch['q']
            ch['x'] = _dot3(_cat([ch['ah_s'], _split(ch['lak'])], 1), _cat([q['hst_s'], q['v_s']], 0), nn)
        for ch in chains:
            ch['u'] = _dot3(_split(ch['tinv']), _split(ch['x']), nn)
        for ch in chains:
            q = ch['q']
            ch['y'] = _dot3(_cat([ch['rh_s'], _split(ch['mrb']), _split(ch['mrk'])], 1),
                            _cat([q['hst_s'], _split(ch['u']), q['v_s']], 0), nn)
        for p, q in enumerate(pairs):
            c0, c1 = chains[2 * p], chains[2 * p + 1]
            y_ref[rows, q['lanes']] = jnp.where(half0, c0['y'], c1['y'])
            u = jnp.where(half0, c0['u'], c1['u'])
            upd = _dot3(_split(q['lhs_t']), _cat([_split(u), q['v_s'], q['hst_s']], 0), nn)
            st_ref[p] = jnp.where(blockdiag, upd, 0.0)
        return carry

    lax.fori_loop(0, nchunk, chunk, 0)

    @pl.when(tb == pl.num_programs(1) - 1)
    def _():
        for h in range(RW_HEADS):
            o = (h % 2) * RW_HD
            hT_ref[0, h] = st_ref[h // 2, o:o + RW_HD, o:o + RW_HD]


def _rw_scan(r, lw, k, v, kk, a, s0, b, t):
    tb = min(512, t)
    nt = t // tb
    row = pl.BlockSpec((tb, RW_W), lambda i, j: (i * nt + j, 0))
    st = pl.BlockSpec((1, RW_HEADS, RW_HD, RW_HD), lambda i, j: (i, 0, 0, 0))
    y, h_end = pl.pallas_call(
        functools.partial(_rw_scan_kernel, nchunk=tb // RW_CHUNK),
        grid=(b, nt),
        in_specs=[row] * 6 + [st],
        out_specs=[row, st],
        out_shape=[jax.ShapeDtypeStruct((b * t, RW_W), F32),
                   jax.ShapeDtypeStruct((b, RW_HEADS, RW_HD, RW_HD), F32)],
        scratch_shapes=[pltpu.VMEM((RW_HEADS // 2, LANES, LANES), F32)],
        compiler_params=_cparams("parallel", "arbitrary"),
        name="rwkv_scan",
    )(r, lw, k, v, kk, a, jnp.swapaxes(s0, -1, -2))
    return y, jnp.swapaxes(h_end, -1, -2)


RW_STEP_SEQS = 8


def _rw_step_kernel(r_ref, lw_ref, k_ref, v_ref, kk_ref, a_ref, s0_ref, seg_ref, y_ref, sT_ref, *, steps):
    nb = s0_ref.shape[0]
    rows = nb * RW_HD
    seg = seg_ref[...]
    st = s0_ref[...].reshape(rows, RW_W)
    vi = lax.broadcasted_iota(jnp.int32, (rows, RW_W), 0) % RW_HD
    ki = lax.broadcasted_iota(jnp.int32, (rows, RW_W), 1) % RW_HD
    dmask = vi == ki
    seg_sum = functools.partial(jnp.dot, precision=HI, preferred_element_type=F32)

    def bc(ref, t):
        return jnp.broadcast_to(ref[:, t:t + 1, :], (nb, RW_HD, RW_W)).reshape(rows, RW_W)

    for t in range(steps):
        kk = bc(kk_ref, t)
        sa = seg_sum(st * (-kk), seg)
        vcol = seg_sum(jnp.where(dmask, bc(v_ref, t), 0.0), seg)
        st = st * jnp.exp(bc(lw_ref, t)) + sa * (kk * bc(a_ref, t)) + vcol * bc(k_ref, t)
        ycol = seg_sum(st * bc(r_ref, t), seg)
        y_ref[:, t:t + 1, :] = jnp.sum(jnp.where(dmask, ycol, 0.0).reshape(nb, RW_HD, RW_W), axis=1, keepdims=True)
    sT_ref[...] = st.reshape(nb, RW_HD, RW_W)


def _rw_steps(r, lw, k, v, kk, a, s0, seg, b, t):
    nb = RW_STEP_SEQS
    row = pl.BlockSpec((nb, t, RW_W), lambda i: (i, 0, 0))
    st = pl.BlockSpec((nb, RW_HD, RW_W), lambda i: (i, 0, 0))
    slab = lambda s: s.transpose(0, 2, 1, 3).reshape(b, RW_HD, RW_W)
    y, s_end = pl.pallas_call(
        functools.partial(_rw_step_kernel, steps=t),
        grid=(b // nb,),
        in_specs=[row] * 6 + [st, pl.BlockSpec((RW_W, RW_W), lambda i: (0, 0))],
        out_specs=[row, st],
        out_shape=[jax.ShapeDtypeStruct((b, t, RW_W), F32), jax.ShapeDtypeStruct((b, RW_HD, RW_W), F32)],
        compiler_params=_cparams("parallel"),
        name="rwkv_steps",
    )(*(u.reshape(b, t, RW_W) for u in (r, lw, k, v, kk, a)), slab(s0), seg)
    return y.reshape(b * t, RW_W), s_end.reshape(b, RW_HD, RW_HEADS, RW_HD).transpose(0, 2, 1, 3)


def _rw_post_kernel(y_ref, r_ref, k_ref, v_ref, g_ref, rk_ref, lg_ref, lb_ref, seg_ref, o_ref):
    seg = seg_ref[...]
    y = y_ref[...]
    mean = jnp.dot(y, seg, precision=HI, preferred_element_type=F32) * (1.0 / RW_HD)
    yc = y - mean
    var = jnp.dot(yc * yc, seg, precision=HI, preferred_element_type=F32) * (1.0 / RW_HD)
    yn = yc * lax.rsqrt(var + RW_LN_EPS) * lg_ref[...] + lb_ref[...]
    v = v_ref[...]
    bonus = jnp.dot(r_ref[...] * k_ref[...] * rk_ref[...], seg, precision=HI, preferred_element_type=F32) * v
    o_ref[...] = ((yn + bonus) * g_ref[...]).astype(BF16)


def _rw_post(y, r, k, v, g, rk, lg, lb, seg):
    m = y.shape[0]
    tm = min(512, m)
    row = pl.BlockSpec((tm, RW_W), lambda i: (i, 0))
    vec = pl.BlockSpec((1, RW_W), lambda i: (0, 0))
    return pl.pallas_call(
        _rw_post_kernel,
        grid=(m // tm,),
        in_specs=[row] * 5 + [vec] * 3 + [pl.BlockSpec((RW_W, RW_W), lambda i: (0, 0))],
        out_specs=row,
        out_shape=jax.ShapeDtypeStruct((m, RW_W), BF16),
        compiler_params=_cparams("parallel"),
        name="rwkv_post",
    )(y, r, k, v, g, rk, lg, lb, seg)


def _s5_mats(lam_re, lam_im, log_step, b_re, b_im, c_re, c_im, cs):
    delta = jnp.exp(log_step)[:, None]
    mag = jnp.exp(lam_re * delta)
    ab_re, ab_im = mag * jnp.cos(lam_im * delta), mag * jnp.sin(lam_im * delta)
    den = lam_re * lam_re + lam_im * lam_im
    n_re, n_im = ab_re - 1.0, ab_im
    q_re = (n_re * lam_re + n_im * lam_im) / den
    q_im = (n_im * lam_re - n_re * lam_im) / den
    bb_re = q_re[..., None] * b_re - q_im[..., None] * b_im
    bb_im = q_re[..., None] * b_im + q_im[..., None] * b_re
    tau = jnp.arange(cs + 1, dtype=F32)[:, None, None]
    pmag = jnp.exp(tau * (lam_re * delta)[None])
    pr, pi = pmag * jnp.cos(tau * (lam_im * delta)[None]), pmag * jnp.sin(tau * (lam_im * delta)[None])
    wr = pr[:cs, :, :, None] * bb_re[None] - pi[:cs, :, :, None] * bb_im[None]
    wi = pr[:cs, :, :, None] * bb_im[None] + pi[:cs, :, :, None] * bb_re[None]
    kern = (jnp.einsum('gcp,tgpd->tgdc', c_re, wr, precision=HI)
            - jnp.einsum('gcp,tgpd->tgdc', c_im, wi, precision=HI))
    lag = jnp.arange(cs)[None, :] - jnp.arange(cs)[:, None]
    toep = jnp.where((lag >= 0)[:, :, None, None, None], kern[jnp.maximum(lag, 0)], 0.0)
    toep = toep.transpose(2, 0, 3, 1, 4).reshape(S5_GROUPS, cs * S5_GC, cs * S5_GC)
    rev = jnp.arange(cs - 1, -1, -1)
    gmat = jnp.concatenate([wr[rev], wi[rev]], axis=2)
    gmat = gmat.transpose(1, 0, 3, 2).reshape(S5_GROUPS, cs * S5_GC, 2 * S5_P)
    p1r, p1i = pr[1:], pi[1:]
    h_re = c_re[None] * p1r[:, :, None, :] - c_im[None] * p1i[:, :, None, :]
    h_im = -c_re[None] * p1i[:, :, None, :] - c_im[None] * p1r[:, :, None, :]
    hmat = jnp.concatenate([h_re, h_im], axis=3)
    hmat = hmat.transpose(1, 3, 0, 2).reshape(S5_GROUPS, 2 * S5_P, cs * S5_GC)
    lam_a = jnp.concatenate([pr[cs], pr[cs]], axis=-1)[:, None, :]
    lam_b = jnp.concatenate([-pi[cs], pi[cs]], axis=-1)[:, None, :]
    return toep, gmat, hmat, lam_a, lam_b


def _s5_in_kernel(u_ref, gm_ref, o_ref):
    o_ref[0] = jnp.dot(u_ref[0], gm_ref[0], precision=HI, preferred_element_type=F32)


def _s5_carry_kernel(g_ref, x0_ref, la_ref, lb_ref, xs_ref, xT_ref, *, nj):
    gt, bsz = x0_ref.shape[0], x0_ref.shape[1]
    la = jnp.broadcast_to(la_ref[...], (gt, bsz, LANES)).reshape(gt * bsz, LANES)
    lb = jnp.broadcast_to(lb_ref[...], (gt, bsz, LANES)).reshape(gt * bsz, LANES)

    def step(j, x):
        xs_ref[:, j] = x.reshape(gt, bsz, LANES)
        return x * la + pltpu.roll(x, S5_P, 1) * lb + g_ref[:, j].reshape(gt * bsz, LANES)

    x = lax.fori_loop(0, nj, step, x0_ref[...].reshape(gt * bsz, LANES))
    xT_ref[...] = x.reshape(gt, bsz, LANES)


def _s5_out_kernel(u_ref, xs_ref, tp_ref, hm_ref, d_ref, o_ref):
    u = u_ref[0]
    y = (jnp.dot(u, tp_ref[0], precision=HI, preferred_element_type=F32)
         + jnp.dot(xs_ref[0], hm_ref[0], precision=HI, preferred_element_type=F32))
    o_ref[0] = y + d_ref[0] * u


def _s5_scan(ug, x0, mats, dg, nj, bsz):
    toep, gmat, hmat, lam_a, lam_b = mats
    ng, rows, w = ug.shape
    grp = lambda r, c: pl.BlockSpec((1, r, c), lambda g: (g, 0, 0))
    gin = pl.pallas_call(
        _s5_in_kernel,
        grid=(ng,),
        in_specs=[grp(rows, w), grp(w, LANES)],
        out_specs=grp(rows, LANES),
        out_shape=jax.ShapeDtypeStruct((ng, rows, LANES), F32),
        compiler_params=_cparams("parallel"),
        name="s5_chunk_in",
    )(ug, gmat)
    gt = 8
    xs, x_last = pl.pallas_call(
        functools.partial(_s5_carry_kernel, nj=nj),
        grid=(ng // gt,),
        in_specs=[pl.BlockSpec((gt, nj, bsz, LANES), lambda g: (g, 0, 0, 0)),
                  pl.BlockSpec((gt, bsz, LANES), lambda g: (g, 0, 0)),
                  pl.BlockSpec((gt, 1, LANES), lambda g: (g, 0, 0)),
                  pl.BlockSpec((gt, 1, LANES), lambda g: (g, 0, 0))],
        out_specs=[pl.BlockSpec((gt, nj, bsz, LANES), lambda g: (g, 0, 0, 0)),
                   pl.BlockSpec((gt, bsz, LANES), lambda g: (g, 0, 0))],
        out_shape=[jax.ShapeDtypeStruct((ng, nj, bsz, LANES), F32),
                   jax.ShapeDtypeStruct((ng, bsz, LANES), F32)],
        compiler_params=_cparams("parallel"),
        name="s5_carry",
    )(gin.reshape(ng, nj, bsz, LANES), x0, lam_a, lam_b)
    yg = pl.pallas_call(
        _s5_out_kernel,
        grid=(ng,),
        in_specs=[grp(rows, w), grp(rows, LANES), grp(w, w), grp(LANES, w), grp(1, w)],
        out_specs=grp(rows, w),
        out_shape=jax.ShapeDtypeStruct((ng, rows, w), F32),
        compiler_params=_cparams("parallel"),
        name="s5_chunk_out",
    )(ug, xs.reshape(ng, rows, LANES), toep, hmat, dg)
    return yg, x_last


def _s5_glu_kernel(y_ref, w1_ref, w2_ref, o_ref):
    y = y_ref[...]
    y = y * (0.5 * (1.0 + jnp.tanh(math.sqrt(2.0 / math.pi) * (y + 0.044715 * (y * y * y)))))
    yb = y.astype(BF16)
    o_ref[...] = (jnp.dot(yb, w1_ref[...], preferred_element_type=F32)
                  * _sigmoid(jnp.dot(yb, w2_ref[...], preferred_element_type=F32))).astype(BF16)


def _s5_glu(y, w1, w2):
    m = y.shape[0]
    tm = min(512, m)
    row = pl.BlockSpec((tm, S5_W), lambda i: (i, 0))
    mat = pl.BlockSpec((S5_W, S5_W), lambda i: (0, 0))
    return pl.pallas_call(
        _s5_glu_kernel,
        grid=(m // tm,),
        in_specs=[row, mat, mat],
        out_specs=row,
        out_shape=jax.ShapeDtypeStruct((m, S5_W), BF16),
        compiler_params=_cparams("parallel"),
        name="s5_glu",
    )(y, w1, w2)


def _s5(zs, x0_re, x0_im, mats, dskip, glu1, glu2, b, t, cs):
    nj = t // cs
    w = cs * S5_GC
    wp = max(w, LANES)
    lpad = lambda m, axes: jnp.pad(m, [(0, wp - w) if ax in axes else (0, 0) for ax in range(m.ndim)])
    toep, gmat, hmat, lam_a, lam_b = mats
    mats = (lpad(toep, (1, 2)), lpad(gmat, (1,)), lpad(hmat, (2,)), lam_a, lam_b)
    ug = zs.reshape(b, nj, cs, S5_GROUPS, S5_GC).transpose(3, 1, 0, 2, 4).reshape(S5_GROUPS, nj * b, w)
    x0 = jnp.concatenate([x0_re, x0_im], axis=-1).transpose(1, 0, 2)
    dg = jnp.tile(dskip.reshape(S5_GROUPS, 1, S5_GC), (1, 1, cs))
    yg, x_last = _s5_scan(lpad(ug, (2,)), x0, mats, lpad(dg, (2,)), nj, b)
    y = yg[:, :, :w].reshape(S5_GROUPS, nj, b, cs, S5_GC).transpose(2, 1, 3, 0, 4).reshape(b * t, S5_W)
    x_last = x_last.transpose(1, 0, 2)
    return _s5_glu(y, glu1, glu2), x_last[..., :S5_P], x_last[..., S5_P:]


def _mix_out_kernel(x_ref, ya_ref, yr_ref, ys_ref, w_ref, o_ref):
    acc = jnp.dot(ya_ref[...].astype(BF16), w_ref[:ATT_W, :], preferred_element_type=F32)
    acc = acc + jnp.dot(yr_ref[...], w_ref[ATT_W:ATT_W + RW_W, :], preferred_element_type=F32)
    acc = acc + jnp.dot(ys_ref[...], w_ref[ATT_W + RW_W:, :], preferred_element_type=F32)
    o_ref[...] = x_ref[...] + acc


def _mix_out(x, ya, yr, ys, w):
    m, d = x.shape
    tm = min(512, m)
    row = lambda c: pl.BlockSpec((tm, c), lambda i: (i, 0))
    return pl.pallas_call(
        _mix_out_kernel,
        grid=(m // tm,),
        in_specs=[row(d), row(ATT_W), row(RW_W), row(S5_W), pl.BlockSpec(w.shape, lambda i: (0, 0))],
        out_specs=row(d),
        out_shape=jax.ShapeDtypeStruct((m, d), F32),
        compiler_params=_cparams("parallel"),
        name="mix_out",
    )(x, ya, yr, ys, w)


def _layer_weights(p, l):
    ffpad = lambda w: jnp.pad(w.astype(BF16), ((0, 0), (0, D_FF_PAD - D_FF)))
    w_in = p['w_in'][l]
    w = {}
    for n in ('ffn1', 'ffn2'):
        w[n] = (p[n + '_norm'][l][None], ffpad(p[n + '_wg'][l]), ffpad(p[n + '_wu'][l]),
                jnp.pad(p[n + '_wd'][l].astype(BF16), ((0, D_FF_PAD - D_FF), (0, 0))))
    w['mix_norm'] = p['mix_norm'][l][None]
    w['w_att'] = jnp.pad(w_in[:, :ATT_IN].astype(BF16), ((0, 0), (0, ATT_Z - ATT_IN)))
    w['w_rw'] = w_in[:, ATT_IN:ATT_IN + RW_IN].astype(BF16)
    w['w_s5'] = w_in[:, ATT_IN + RW_IN:].astype(BF16)
    w['q_norm'] = p['q_norm'][l][None]
    w['k_norm'] = p['k_norm'][l][None]
    tailpad = lambda m, off: jnp.pad(m, ((off, LANES - off - m.shape[0]), (0, 0)))
    w['rw_prep'] = (p['rw_mu'][l][None], p['rw_w0'][l][None], p['rw_a0'][l][None], p['rw_kk'][l][None],
                    p['rw_ka'][l][None], tailpad(p['rw_w2'][l], 0), tailpad(p['rw_a2'][l], RW_DECAY_R),
                    tailpad(p['rw_g2'][l], RW_DECAY_R + RW_A_R), _head_ones())
    w['rw_post'] = (p['rw_rk'][l].reshape(1, RW_W), p['rw_lnx_g'][l][None], p['rw_lnx_b'][l][None], _head_ones())
    w['s5_params'] = (p['s5_lam_re'][l], p['s5_lam_im'][l], p['s5_log_step'][l], p['s5_b_re'][l], p['s5_b_im'][l],
                      p['s5_c_re'][l], p['s5_c_im'][l])
    w['s5_d'] = p['s5_d'][l]
    w['s5_glu'] = (p['s5_glu1'][l].astype(BF16), p['s5_glu2'][l].astype(BF16))
    w['w_out'] = p['w_out'][l].astype(BF16)
    return w


def _layer(x, w, tabs, b, t, attn_fn, shift0, rw0, s5re0, s5im0, s5_chunk):
    m = b * t
    x = _ffn(x, *w['ffn1'])
    za = _nmm(x, w['mix_norm'], w['w_att'], ATT_Z // 3)
    zr = _nmm(x, w['mix_norm'], w['w_rw'], RW_IN)
    zs = _nmm(x, w['mix_norm'], w['w_s5'], S5_W)
    qb, k32, kb, vb, qib, ki32, kid, wi = _attn_prep(za, tabs, w['q_norm'], w['k_norm'])
    v32 = za[:, 2 * ATT_W:3 * ATT_W]
    ya = attn_fn(qb, kb, vb, v32, qib, kid, wi)
    zr3 = zr.reshape(b, t, RW_IN)
    prev = jnp.concatenate([shift0[:, None, :], zr3[:, :-1]], axis=1).reshape(m, RW_IN)
    r, lw, k2, v, kk, a, g = _rw_prep(zr, prev, *w['rw_prep'])
    if t % RW_CHUNK == 0:
        y, rw_new = _rw_scan(r, lw, k2, v, kk, a, rw0, b, t)
    else:
        y, rw_new = _rw_steps(r, lw, k2, v, kk, a, rw0, w['rw_post'][3], b, t)
    yr = _rw_post(y, r, k2, v, g, *w['rw_post'])
    mats = _s5_mats(*w['s5_params'], s5_chunk)
    ys, s5re, s5im = _s5(zs, s5re0, s5im0, mats, w['s5_d'], *w['s5_glu'], b, t, s5_chunk)
    x = _mix_out(x, ya, yr, ys, w['w_out'])
    x = _ffn(x, *w['ffn2'])
    state = (k32.reshape(b, t, ATT_HEADS, ATT_HD), v32.reshape(b, t, ATT_HEADS, ATT_HD),
             ki32.reshape(b, t, IDX_HD), rw_new, zr3[:, -1], s5re, s5im)
    return x, state


def kernel(x_prompt, x_sample, cache_k, cache_v, cache_kidx, state_rwkv, state_shift, state_s5_re, state_s5_im, page_table, ffn1_norm, ffn1_wg, ffn1_wu, ffn1_wd, mix_norm, w_in, q_norm, k_norm, rw_mu, rw_w0, rw_w2, rw_a0, rw_a2, rw_g2, rw_kk, rw_ka, rw_rk, rw_lnx_g, rw_lnx_b, s5_lam_re, s5_lam_im, s5_log_step, s5_b_re, s5_b_im, s5_c_re, s5_c_im, s5_d, s5_glu1, s5_glu2, w_out, ffn2_norm, ffn2_wg, ffn2_wu, ffn2_wd):
    p = dict(ffn1_norm=ffn1_norm, ffn1_wg=ffn1_wg, ffn1_wu=ffn1_wu, ffn1_wd=ffn1_wd, mix_norm=mix_norm, w_in=w_in,
             q_norm=q_norm, k_norm=k_norm, rw_mu=rw_mu, rw_w0=rw_w0, rw_w2=rw_w2, rw_a0=rw_a0, rw_a2=rw_a2,
             rw_g2=rw_g2, rw_kk=rw_kk, rw_ka=rw_ka, rw_rk=rw_rk, rw_lnx_g=rw_lnx_g, rw_lnx_b=rw_lnx_b,
             s5_lam_re=s5_lam_re, s5_lam_im=s5_lam_im, s5_log_step=s5_log_step, s5_b_re=s5_b_re, s5_b_im=s5_b_im,
             s5_c_re=s5_c_re, s5_c_im=s5_c_im, s5_d=s5_d, s5_glu1=s5_glu1, s5_glu2=s5_glu2, w_out=w_out,
             ffn2_norm=ffn2_norm, ffn2_wg=ffn2_wg, ffn2_wu=ffn2_wu, ffn2_wd=ffn2_wd)
    depth = w_in.shape[0]
    b, t, d = x_prompt.shape
    db, s, _ = x_sample.shape
    past = page_table.shape[1] * PAGE_SIZE
    weights = [_layer_weights(p, l) for l in range(depth)]

    tabs_p = _rope_tables(jnp.arange(t))
    attn_p = lambda qb, kb, vb, v32, qib, kid, wi: _dsa_prompt(qb, kb, vb, qib, kid, wi, b, t)
    y = x_prompt.reshape(b * t, d)
    st_p = []
    for l in range(depth):
        y, st = _layer(y, weights[l], tabs_p, b, t, attn_p, jnp.zeros((b, RW_IN), F32),
                       jnp.zeros((b, RW_HEADS, RW_HD, RW_HD), F32), jnp.zeros((b, S5_GROUPS, S5_P), F32),
                       jnp.zeros((b, S5_GROUPS, S5_P), F32), s5_chunk=16)
        st_p.append(st)
    y_prompt = y.reshape(b, t, d)

    ms = db * s
    tm_s = min(512, ms)
    tabs_s = tuple(jnp.tile(tb, (tm_s // s, 1)) for tb in _rope_tables(past + jnp.arange(s)))
    y = x_sample.reshape(ms, d)
    st_s = []
    for l in range(depth):
        attn_s = functools.partial(
            lambda qb, kb, vb, v32, qib, kid, wi, l: _dsa_sample(qb, kb, v32, qib, kid, wi, cache_k, cache_v,
                                                                  cache_kidx, l, page_table, db, s), l=l)
        y, st = _layer(y, weights[l], tabs_s, db, s, attn_s, state_shift[l], state_rwkv[l], state_s5_re[l],
                       state_s5_im[l], s5_chunk=s)
        st_s.append(st)
    y_sample = y.reshape(db, s, d)
    outs_p = [jnp.stack(u) for u in zip(*st_p)]
    outs_s = [jnp.stack(u) for u in zip(*st_s)]
    return (y_prompt, y_sample, *outs_p, *outs_s)
```

```python
import functools
import math

import jax
import jax.numpy as jnp
from jax import lax
from jax.experimental import pallas as pl
from jax.experimental.pallas import tpu as pltpu

F32 = jnp.float32
BF16 = jnp.bfloat16
HI = lax.Precision.HIGHEST

D_MODEL = 2048
PAGE_SIZE = 128
ATT_HD = 128
ATT_W = 1024
ATT_HEADS = 8
IDX_HEADS = 16
IDX_HD = 64
IDX_W = IDX_HEADS * IDX_HD
IDX_WX = IDX_HEADS * 128
TOPK_MAX = 256
ROPE_THETA = 10000.0
RW_HD = 64
RW_W = 512
RW_HEADS = 8
RW_DECAY_R = 32
RW_A_R = 32
RW_G_R = 64
RW_IN = 3 * RW_W + RW_DECAY_R + RW_A_R + RW_G_R
S5_W = 512
S5_GC = 16
S5_GROUPS = 32
S5_P = 64
ATT_IN = 3 * ATT_W + IDX_W + IDX_HD + IDX_HEADS
D_FF = 5504
NORM_EPS = 1e-6
RW_LN_EPS = 64e-5

LANES = 128
VMEM_LIMIT = 52 * 1024 * 1024
FF_TILE = 512
D_FF_PAD = ((D_FF + FF_TILE - 1) // FF_TILE) * FF_TILE
ATT_Z = 4 * ATT_W + LANES
INT_MIN = -2147483648
NEG_INF = float("-inf")


def _cparams(*sem):
    return pltpu.CompilerParams(dimension_semantics=sem, vmem_limit_bytes=VMEM_LIMIT)


def _nt(a, b, precision=None):
    return lax.dot_general(a, b, (((1,), (1,)), ((), ())), precision=precision,
                           preferred_element_type=F32)


def _tn(a, b, precision=None):
    return lax.dot_general(a, b, (((0,), (0,)), ((), ())), precision=precision,
                           preferred_element_type=F32)


def _split(a):
    hi = a.astype(BF16)
    return hi, (a - hi.astype(F32)).astype(BF16)


def _cat(parts, axis):
    return tuple(jnp.concatenate([p[i] for p in parts], axis=axis) for i in range(2))


def _dot3(a, b, dims):
    dg = lambda x, y: lax.dot_general(x, y, dims, preferred_element_type=F32)
    return dg(a[0], b[0]) + dg(a[0], b[1]) + dg(a[1], b[0])


NN_DIMS = (((1,), (0,)), ((), ()))


def _seg_sum(x, seg):
    hi, lo = _split(x)
    return (jnp.dot(hi, seg, preferred_element_type=F32) + jnp.dot(lo, seg, preferred_element_type=F32))


def _sigmoid(x):
    return 1.0 / (1.0 + jnp.exp(-x))


def _rms(x, g):
    ms = jnp.mean(x * x, axis=-1, keepdims=True)
    return x * lax.rsqrt(ms + NORM_EPS) * g


def _ffn_kernel(x_ref, g_ref, wg_ref, wu_ref, wd_ref, o_ref, xn_ref):
    j = pl.program_id(1)

    @pl.when(j == 0)
    def _():
        xn_ref[...] = _rms(x_ref[...], g_ref[...]).astype(BF16)
        o_ref[...] = jnp.zeros_like(o_ref)

    xn = xn_ref[...]
    hg = jnp.dot(xn, wg_ref[...], preferred_element_type=F32)
    hu = jnp.dot(xn, wu_ref[...], preferred_element_type=F32)
    h = (hg * _sigmoid(hg) * hu).astype(BF16)
    o_ref[...] += jnp.dot(h, wd_ref[...], preferred_element_type=F32)

    @pl.when(j == pl.num_programs(1) - 1)
    def _():
        o_ref[...] = x_ref[...] + 0.5 * o_ref[...]


def _ffn(x, g, wg, wu, wd):
    m, d = x.shape
    tm = min(512, m)
    nf = wg.shape[1] // FF_TILE
    return pl.pallas_call(
        _ffn_kernel,
        grid=(m // tm, nf),
        in_specs=[
            pl.BlockSpec((tm, d), lambda i, j: (i, 0)),
            pl.BlockSpec((1, d), lambda i, j: (0, 0)),
            pl.BlockSpec((d, FF_TILE), lambda i, j: (0, j)),
            pl.BlockSpec((d, FF_TILE), lambda i, j: (0, j)),
            pl.BlockSpec((FF_TILE, d), lambda i, j: (j, 0)),
        ],
        out_specs=pl.BlockSpec((tm, d), lambda i, j: (i, 0)),
        out_shape=jax.ShapeDtypeStruct((m, d), F32),
        scratch_shapes=[pltpu.VMEM((tm, d), BF16)],
        compiler_params=_cparams("parallel", "arbitrary"),
        name="ffn",
    )(x, g, wg, wu, wd)


def _nmm_kernel(x_ref, g_ref, w_ref, o_ref, xn_ref):
    @pl.when(pl.program_id(1) == 0)
    def _():
        xn_ref[...] = _rms(x_ref[...], g_ref[...]).astype(BF16)

    o_ref[...] = jnp.dot(xn_ref[...], w_ref[...], preferred_element_type=F32)


def _nmm(x, g, w, tn):
    m, d = x.shape
    n = w.shape[1]
    tm = min(512, m)
    return pl.pallas_call(
        _nmm_kernel,
        grid=(m // tm, n // tn),
        in_specs=[
            pl.BlockSpec((tm, d), lambda i, j: (i, 0)),
            pl.BlockSpec((1, d), lambda i, j: (0, 0)),
            pl.BlockSpec((d, tn), lambda i, j: (0, j)),
        ],
        out_specs=pl.BlockSpec((tm, tn), lambda i, j: (i, j)),
        out_shape=jax.ShapeDtypeStruct((m, n), F32),
        scratch_shapes=[pltpu.VMEM((tm, d), BF16)],
        compiler_params=_cparams("parallel", "arbitrary"),
        name="norm_proj",
    )(x, g, w)


def _rope_tables(pos):
    def tab(half, reps):
        inv = ROPE_THETA ** (-jnp.arange(half, dtype=F32) / half)
        ang = pos.astype(F32)[:, None] * inv[None, :]
        c, s = jnp.cos(ang), jnp.sin(ang)
        return jnp.tile(jnp.concatenate([c, c], -1), (1, reps)), jnp.tile(jnp.concatenate([-s, s], -1), (1, reps))
    c128, s128 = tab(ATT_HD // 2, 1)
    c64, s64 = tab(IDX_HD // 2, 2)
    return c128, s128, c64, s64


def _rope128(x, c, s):
    return x * c + pltpu.roll(x, ATT_HD // 2, 1) * s


def _rope64(x, c, s):
    lane = lax.broadcasted_iota(jnp.int32, x.shape, 1)
    first = (lane % IDX_HD) < IDX_HD // 2
    partner = jnp.where(first, pltpu.roll(x, LANES - IDX_HD // 2, 1), pltpu.roll(x, IDX_HD // 2, 1))
    return x * c + partner * s


def _prep_kernel(q_ref, k_ref, v_ref, qi_ref, kw_ref, c128_ref, s128_ref, c64_ref, s64_ref, qn_ref, kn_ref,
                 qo_ref, ko_ref, kb_ref, vo_ref, vb_ref, qio_ref, kio_ref, kid_ref, wio_ref):
    c128, s128 = c128_ref[...], s128_ref[...]
    c64, s64 = c64_ref[...], s64_ref[...]
    for h in range(ATT_HEADS):
        sl = slice(h * ATT_HD, (h + 1) * ATT_HD)
        q = _rope128(_rms(q_ref[:, sl], qn_ref[...]), c128, s128)
        k = _rope128(_rms(k_ref[:, sl], kn_ref[...]), c128, s128)
        qo_ref[:, sl] = q.astype(BF16)
        ko_ref[:, sl] = k
        kb_ref[:, sl] = k.astype(BF16)
    v = v_ref[...]
    vo_ref[...] = v
    vb_ref[...] = v.astype(BF16)
    lane = lax.broadcasted_iota(jnp.int32, c64.shape, 1)
    for p in range(IDX_W // LANES):
        qi = _rope64(qi_ref[:, p * LANES:(p + 1) * LANES], c64, s64)
        qio_ref[:, 2 * p * LANES:(2 * p + 1) * LANES] = jnp.where(lane < IDX_HD, qi, 0.0).astype(BF16)
        qio_ref[:, (2 * p + 1) * LANES:(2 * p + 2) * LANES] = jnp.where(lane < IDX_HD, 0.0, qi).astype(BF16)
    kw = kw_ref[...]
    ki = _rope64(kw, c64, s64)
    kio_ref[...] = ki[:, :IDX_HD]
    kid_ref[...] = jnp.where(lane < IDX_HD, ki, pltpu.roll(ki, IDX_HD, 1)).astype(BF16)
    wio_ref[...] = kw[:, IDX_HD:IDX_HD + IDX_HEADS] * (IDX_W ** -0.5)


def _attn_prep(za, tabs, q_norm, k_norm):
    m = za.shape[0]
    tm = min(512, m)
    nt = tabs[0].shape[0] // tm
    wide = lambda c: pl.BlockSpec((tm, ATT_W), lambda i: (i, c))
    tab = pl.BlockSpec((tm, LANES), lambda i: (i % nt, 0))
    vec = pl.BlockSpec((1, ATT_HD), lambda i: (0, 0))
    row = lambda w: pl.BlockSpec((tm, w), lambda i: (i, 0))
    sds = lambda w, dt: jax.ShapeDtypeStruct((m, w), dt)
    return pl.pallas_call(
        _prep_kernel,
        grid=(m // tm,),
        in_specs=[wide(0), wide(1), wide(2), wide(3),
                  pl.BlockSpec((tm, LANES), lambda i: (i, 4 * ATT_W // LANES)),
                  tab, tab, tab, tab, vec, vec],
        out_specs=[row(ATT_W), row(ATT_W), row(ATT_W), row(ATT_W), row(ATT_W), row(IDX_WX), row(IDX_HD), row(LANES),
                   row(IDX_HEADS)],
        out_shape=[sds(ATT_W, BF16), sds(ATT_W, F32), sds(ATT_W, BF16), sds(ATT_W, F32), sds(ATT_W, BF16),
                   sds(IDX_WX, BF16),
                   sds(IDX_HD, F32), sds(LANES, BF16), sds(IDX_HEADS, F32)],
        compiler_params=_cparams("parallel"),
        name="attn_prep",
    )(za, za, za, za, za, *tabs, q_norm, k_norm)


def _order_key(x):
    bits = lax.bitcast_convert_type(x, jnp.int32)
    return bits ^ ((bits >> 31) & jnp.int32(0x7FFFFFFF))


def _kth_largest(count_ge, rows, k):
    def body(i, cur):
        cand = cur + (jnp.int32(1) << (31 - i))
        return jnp.where(count_ge(cand) >= k, cand, cur)
    return lax.fori_loop(0, 32, body, jnp.full((rows, 1), INT_MIN, jnp.int32))


def _dsa_prompt_kernel(q_ref, k_ref, v_ref, qi_ref, kid_ref, wi_ref, o_ref, key_ref, bias_ref, *, tq, t0, topk):
    tk = t0 + tq
    kid = kid_ref[0]
    wi = wi_ref[...]
    acc = jnp.zeros((tq, tk), F32)
    for h in range(IDX_HEADS):
        acc = acc + jnp.maximum(_nt(qi_ref[:, h * LANES:(h + 1) * LANES], kid), 0.0) * wi[:, h:h + 1]
    col = lax.broadcasted_iota(jnp.int32, (tq, tk), 1)
    row = t0 + lax.broadcasted_iota(jnp.int32, (tq, tk), 0)
    causal = col <= row
    key_ref[...] = _order_key(jnp.where(causal, acc, NEG_INF))

    def count_ge(cand):
        return jnp.sum(jnp.where(key_ref[...] >= cand, 1.0, 0.0), axis=-1, keepdims=True)

    thr = _kth_largest(count_ge, tq, topk)
    bias_ref[...] = jnp.where((key_ref[...] >= thr) & causal, 0.0, NEG_INF)
    scale = ATT_HD ** -0.5
    for h in range(ATT_HEADS):
        sl = slice(h * ATT_HD, (h + 1) * ATT_HD)
        s = _nt(q_ref[:, sl], k_ref[0, :, sl]) * scale + bias_ref[...]
        mx = jnp.max(s, axis=-1, keepdims=True)
        pr = jnp.exp(s - mx)
        den = jnp.sum(pr, axis=-1, keepdims=True)
        o_ref[:, sl] = jnp.dot(pr.astype(BF16), v_ref[0, :, sl], preferred_element_type=F32) / den


def _dsa_prompt(qb, kb, vb, qib, kid, wi, b, t):
    topk = min(TOPK_MAX, t // 4)
    tq = min(256, t)
    nq = t // tq
    k3, v3, kid3 = kb.reshape(b, t, ATT_W), vb.reshape(b, t, ATT_W), kid.reshape(b, t, LANES)
    outs = []
    for j in range(nq):
        tk = (j + 1) * tq
        blk = lambda w, j=j: pl.BlockSpec((tq, w), lambda i: (i * nq + j, 0))
        keys = lambda w, tk=tk: pl.BlockSpec((1, tk, w), lambda i: (i, 0, 0))
        outs.append(pl.pallas_call(
            functools.partial(_dsa_prompt_kernel, tq=tq, t0=j * tq, topk=topk),
            grid=(b,),
            in_specs=[blk(ATT_W), keys(ATT_W), keys(ATT_W), blk(IDX_WX), keys(LANES), blk(IDX_HEADS)],
            out_specs=pl.BlockSpec((tq, ATT_W), lambda i: (i, 0)),
            out_shape=jax.ShapeDtypeStruct((b * tq, ATT_W), F32),
            scratch_shapes=[pltpu.VMEM((tq, tk), jnp.int32), pltpu.VMEM((tq, tk), F32)],
            compiler_params=_cparams("parallel"),
            name="dsa_prompt",
        )(qb, k3, v3, qib, kid3, wi))
    return jnp.stack([o.reshape(b, tq, ATT_W) for o in outs], axis=1).reshape(b * t, ATT_W)


QS = 8
NS = 16


def _dsa_sample_kernel(pt_ref, q_ref, knew_ref, vnew_ref, a_ref, wcol_ref, kinew_ref, ck_hbm, cv_hbm, cki_hbm,
                       o_ref, kbuf, vbuf, kibuf, sem, *, layer, n_pages, s_new, topk):
    b = pl.program_id(0)
    nb = pl.num_programs(0)
    slot = b % 2
    past = n_pages * PAGE_SIZE

    def copies(bb, sl):
        out = []
        for p in range(n_pages):
            page = pt_ref[bb, p]
            dst = pl.ds(p * PAGE_SIZE, PAGE_SIZE)
            for h in range(ATT_HEADS):
                win = pl.ds(h * ATT_HD, ATT_HD)
                out.append(pltpu.make_async_copy(ck_hbm.at[layer, page, :, h], kbuf.at[sl, dst, win], sem.at[0, sl]))
                out.append(pltpu.make_async_copy(cv_hbm.at[layer, page, :, h], vbuf.at[sl, dst, win], sem.at[1, sl]))
            out.append(pltpu.make_async_copy(cki_hbm.at[layer, page], kibuf.at[sl, dst], sem.at[2, sl]))
        return out

    @pl.when(b == 0)
    def _():
        for c in copies(0, 0):
            c.start()

    @pl.when(b + 1 < nb)
    def _():
        for c in copies(b + 1, 1 - slot):
            c.start()

    for c in copies(b, slot):
        c.wait()

    a = a_ref[0]
    wcol = wcol_ref[0]
    ki = kibuf[slot].astype(BF16)
    ki = jnp.concatenate([ki, jnp.zeros_like(ki)], axis=1)
    ip = (jnp.maximum(_nt(a, ki), 0.0) * wcol).reshape(IDX_HEADS, QS, past).sum(axis=0)
    inw = (jnp.maximum(_nt(a, kinew_ref[0]), 0.0) * wcol).reshape(IDX_HEADS, QS, NS).sum(axis=0)
    qrow = lax.broadcasted_iota(jnp.int32, (QS, NS), 0)
    jcol = lax.broadcasted_iota(jnp.int32, (QS, NS), 1)
    new_ok = (jcol <= qrow) & (jcol < s_new)
    key_p = _order_key(ip)
    key_n = _order_key(jnp.where(new_ok, inw, NEG_INF))

    def count_ge(cand):
        return (jnp.sum(jnp.where(key_p >= cand, 1.0, 0.0), axis=-1, keepdims=True)
                + jnp.sum(jnp.where(key_n >= cand, 1.0, 0.0), axis=-1, keepdims=True))

    thr = _kth_largest(count_ge, QS, topk)
    bias_p = jnp.where(key_p >= thr, 0.0, NEG_INF)
    bias_n = jnp.where((key_n >= thr) & new_ok, 0.0, NEG_INF)

    scale = ATT_HD ** -0.5
    qbd = q_ref[0]
    sp = (_nt(qbd, kbuf[slot].astype(BF16)) * scale).reshape(ATT_HEADS, QS, past) + bias_p[None]
    sn = (_nt(qbd, knew_ref[0]) * scale).reshape(ATT_HEADS, QS, NS) + bias_n[None]
    mx = jnp.maximum(jnp.max(sp, axis=-1, keepdims=True), jnp.max(sn, axis=-1, keepdims=True))
    pp = jnp.exp(sp - mx)
    pn = jnp.exp(sn - mx)
    den = jnp.sum(pp, axis=-1, keepdims=True) + jnp.sum(pn, axis=-1, keepdims=True)
    o = jnp.dot(pp.reshape(ATT_HEADS * QS, past).astype(BF16), vbuf[slot].astype(BF16), preferred_element_type=F32)
    pn2 = pn.reshape(ATT_HEADS * QS, NS)
    vnew = vnew_ref[0]
    for j in range(s_new):
        o = o + pn2[:, j:j + 1] * vnew[j:j + 1, :]
    o = o / den.reshape(ATT_HEADS * QS, 1)
    for h in range(ATT_HEADS):
        o_ref[0, :, h * ATT_HD:(h + 1) * ATT_HD] = o[h * QS:h * QS + s_new, h * ATT_HD:(h + 1) * ATT_HD]


def _dsa_sample(qb, kb, v32, qib, kid, wi, ck, cv, cki, layer, page_table, db, s):
    n_pages = page_table.shape[1]
    past = n_pages * PAGE_SIZE
    topk = min(TOPK_MAX, (past + s) // 4)
    q4 = jnp.pad(qb.reshape(db, s, ATT_HEADS, ATT_HD), ((0, 0), (0, QS - s), (0, 0), (0, 0)))
    eye = jnp.eye(ATT_HEADS, dtype=BF16)
    qbd = (q4.transpose(0, 2, 1, 3)[:, :, :, None, :] * eye[None, :, None, :, None]).reshape(db, ATT_HEADS * QS, ATT_W)
    knew = jnp.pad(kb.reshape(db, s, ATT_W), ((0, 0), (0, NS - s), (0, 0)))
    vnew = jnp.pad(v32.reshape(db, s, ATT_W), ((0, 0), (0, NS - s), (0, 0)))
    a = qib.reshape(db, s, IDX_HEADS, 2, IDX_HD).sum(axis=3)
    a = jnp.pad(a, ((0, 0), (0, QS - s), (0, 0), (0, LANES - IDX_HD)))
    a = a.transpose(0, 2, 1, 3).reshape(db, IDX_HEADS * QS, LANES)
    wcol = jnp.pad(wi.reshape(db, s, IDX_HEADS), ((0, 0), (0, QS - s), (0, 0)))
    wcol = wcol.transpose(0, 2, 1).reshape(db, IDX_HEADS * QS, 1)
    kinew = jnp.pad(kid.reshape(db, s, LANES)[:, :, :IDX_HD], ((0, 0), (0, NS - s), (0, LANES - IDX_HD)))
    per = lambda r, w: pl.BlockSpec((1, r, w), lambda i, pt: (i, 0, 0))
    hbm = pl.BlockSpec(memory_space=pl.ANY)
    grid_spec = pltpu.PrefetchScalarGridSpec(
        num_scalar_prefetch=1,
        grid=(db,),
        in_specs=[per(ATT_HEADS * QS, ATT_W), per(NS, ATT_W), per(NS, ATT_W), per(IDX_HEADS * QS, LANES),
                  per(IDX_HEADS * QS, 1), per(NS, LANES), hbm, hbm, hbm],
        out_specs=per(s, ATT_W),
        scratch_shapes=[pltpu.VMEM((2, past, ATT_W), F32), pltpu.VMEM((2, past, ATT_W), F32),
                        pltpu.VMEM((2, past, IDX_HD), F32), pltpu.SemaphoreType.DMA((3, 2))],
    )
    out = pl.pallas_call(
        functools.partial(_dsa_sample_kernel, layer=layer, n_pages=n_pages, s_new=s, topk=topk),
        grid_spec=grid_spec,
        out_shape=jax.ShapeDtypeStruct((db, s, ATT_W), F32),
        compiler_params=_cparams("arbitrary"),
        name="dsa_sample",
    )(page_table, qbd, knew, vnew, a, wcol, kinew, ck, cv, cki)
    return out.reshape(db * s, ATT_W)


def _head_ones():
    i = jnp.arange(RW_W) // RW_HD
    return (i[:, None] == i[None, :]).astype(BF16)


def _rw_prep_kernel(z_ref, p_ref, mu_ref, w0_ref, a0_ref, kkg_ref, ka_ref, w2_ref, a2_ref, g2_ref, seg_ref,
                    r_ref, lw_ref, k_ref, v_ref, kk_ref, a_ref, g_ref):
    z = z_ref[...]
    xm = z + (p_ref[...] - z) * mu_ref[...]
    r = xm[:, :RW_W]
    k = xm[:, RW_W:2 * RW_W]
    tail = xm[:, 3 * RW_W:]
    dw = jnp.dot(jnp.tanh(tail), w2_ref[...], precision=HI, preferred_element_type=F32)
    da = jnp.dot(tail, a2_ref[...], precision=HI, preferred_element_type=F32)
    lw_ref[...] = -_sigmoid(w0_ref[...] + dw) * math.exp(-0.5)
    a = _sigmoid(a0_ref[...] + da)
    g_ref[...] = jnp.dot(_sigmoid(tail), g2_ref[...], precision=HI, preferred_element_type=F32)
    kk = k * kkg_ref[...]
    ss = _seg_sum(kk * kk, seg_ref[...])
    kk_ref[...] = kk / jnp.maximum(jnp.sqrt(ss), 1e-12)
    k_ref[...] = k * (1.0 + (a - 1.0) * ka_ref[...])
    r_ref[...] = r
    v_ref[...] = xm[:, 2 * RW_W:3 * RW_W]
    a_ref[...] = a


def _rw_prep(zr, prev, mu, w0, a0, kkg, ka, w2f, a2f, g2f, seg):
    m = zr.shape[0]
    tm = min(512, m)
    row = lambda w: pl.BlockSpec((tm, w), lambda i: (i, 0))
    const = lambda r, w: pl.BlockSpec((r, w), lambda i: (0, 0))
    return pl.pallas_call(
        _rw_prep_kernel,
        grid=(m // tm,),
        in_specs=[row(RW_IN), row(RW_IN), const(1, RW_IN)] + [const(1, RW_W)] * 4
                 + [const(LANES, RW_W)] * 3 + [const(RW_W, RW_W)],
        out_specs=[row(RW_W)] * 7,
        out_shape=[jax.ShapeDtypeStruct((m, RW_W), F32)] * 7,
        compiler_params=_cparams("parallel"),
        name="rwkv_prep",
    )(zr, prev, mu, w0, a0, kkg, ka, w2f, a2f, g2f, seg)


RW_CHUNK = LANES


def _rw_scan_kernel(r_ref, lw_ref, k_ref, v_ref, kk_ref, a_ref, h0_ref, y_ref, hT_ref, st_ref, *, nchunk):
    c = RW_CHUNK
    npair = RW_HEADS // 2
    tb = pl.program_id(1)

    @pl.when(tb == 0)
    def _():
        st_ref[...] = jnp.zeros_like(st_ref)
        for h in range(RW_HEADS):
            o = (h % 2) * RW_HD
            st_ref[h // 2, o:o + RW_HD, o:o + RW_HD] = h0_ref[0, h]

    ri = lax.broadcasted_iota(jnp.int32, (c, c), 0)
    ci = lax.broadcasted_iota(jnp.int32, (c, c), 1)
    low_incl = ci <= ri
    low_strict = ci < ri
    diag = ci == ri
    tri = jnp.where(low_incl, 1.0, 0.0)
    eye = jnp.where(diag, 1.0, 0.0)
    half0 = ci < RW_HD
    blockdiag = (ri // RW_HD) == (ci // RW_HD)
    nsq = int(math.log2(c))
    nn = (((1,), (0,)), ((), ()))
    nt = (((1,), (1,)), ((), ()))

    def chunk(ic, carry):
        rows = pl.ds(pl.multiple_of(ic * c, c), c)
        pairs = []
        for p in range(npair):
            lanes = slice(p * LANES, (p + 1) * LANES)
            lw = lw_ref[rows, lanes]
            r = r_ref[rows, lanes]
            k = k_ref[rows, lanes]
            v = v_ref[rows, lanes]
            kk = kk_ref[rows, lanes]
            a = a_ref[rows, lanes]
            cum = jnp.dot(tri, lw, precision=HI, preferred_element_type=F32)
            tot = cum[c - 1:c, :]
            pinv = jnp.exp(-cum)
            abar = -kk * jnp.exp(cum - lw)
            rbar = r * jnp.exp(cum)
            pend = jnp.exp(tot - cum)
            hst = st_ref[p]
            decay = jnp.where(diag, jnp.exp(tot), 0.0)
            pairs.append(dict(lanes=lanes, abar=abar, rbar=rbar, bt_s=_split(kk * a * pinv), kt_s=_split(k * pinv),
                              v_s=_split(v), hst_s=_split(hst),
                              lhs_t=jnp.concatenate([kk * a * pend, k * pend, decay], axis=0).T))
        chains = []
        for q in pairs:
            for hf in range(2):
                msk = half0 if hf == 0 else ~half0
                ah_s = _split(jnp.where(msk, q['abar'], 0.0))
                rh_s = _split(jnp.where(msk, q['rbar'], 0.0))
                lab = jnp.where(low_strict, _dot3(ah_s, q['bt_s'], nt), 0.0)
                chains.append(dict(q=q, ah_s=ah_s, rh_s=rh_s, tinv=eye + lab, pw=lab,
                                   lak=jnp.where(low_strict, _dot3(ah_s, q['kt_s'], nt), 0.0),
                                   mrb=jnp.where(low_incl, _dot3(rh_s, q['bt_s'], nt), 0.0),
                                   mrk=jnp.where(low_incl, _dot3(rh_s, q['kt_s'], nt), 0.0)))
        for _ in range(nsq - 1):
            for ch in chains:
                pw_s = _split(ch['pw'])
                ch['pw'] = _dot3(pw_s, pw_s, nn)
            for ch in chains:
                ch['tinv'] = _dot3(_split(ch['tinv']), _split(eye + ch['pw']), nn)
        for ch in chains:
            q = ch['q']
            ch['x'] = _dot3(_cat([ch['ah_s'], _split(ch['lak'])], 1), _cat([q['hst_s'], q['v_s']], 0), nn)
        for ch in chains:
            ch['u'] = _dot3(_split(ch['tinv']), _split(ch['x']), nn)
        for ch in chains:
            q = ch['q']
            ch['y'] = _dot3(_cat([ch['rh_s'], _split(ch['mrb']), _split(ch['mrk'])], 1),
                            _cat([q['hst_s'], _split(ch['u']), q['v_s']], 0), nn)
        for p, q in enumerate(pairs):
            c0, c1 = chains[2 * p], chains[2 * p + 1]
            y_ref[rows, q['lanes']] = jnp.where(half0, c0['y'], c1['y'])
            u = jnp.where(half0, c0['u'], c1['u'])
            upd = _dot3(_split(q['lhs_t']), _cat([_split(u), q['v_s'], q['hst_s']], 0), nn)
            st_ref[p] = jnp.where(blockdiag, upd, 0.0)
        return carry

    lax.fori_loop(0, nchunk, chunk, 0)

    @pl.when(tb == pl.num_programs(1) - 1)
    def _():
        for h in range(RW_HEADS):
            o = (h % 2) * RW_HD
            hT_ref[0, h] = st_ref[h // 2, o:o + RW_HD, o:o + RW_HD]


def _rw_scan(r, lw, k, v, kk, a, s0, b, t):
    tb = min(512, t)
    nt = t // tb
    row = pl.BlockSpec((tb, RW_W), lambda i, j: (i * nt + j, 0))
    st = pl.BlockSpec((1, RW_HEADS, RW_HD, RW_HD), lambda i, j: (i, 0, 0, 0))
    y, h_end = pl.pallas_call(
        functools.partial(_rw_scan_kernel, nchunk=tb // RW_CHUNK),
        grid=(b, nt),
        in_specs=[row] * 6 + [st],
        out_specs=[row, st],
        out_shape=[jax.ShapeDtypeStruct((b * t, RW_W), F32),
                   jax.ShapeDtypeStruct((b, RW_HEADS, RW_HD, RW_HD), F32)],
        scratch_shapes=[pltpu.VMEM((RW_HEADS // 2, LANES, LANES), F32)],
        compiler_params=_cparams("parallel", "arbitrary"),
        name="rwkv_scan",
    )(r, lw, k, v, kk, a, jnp.swapaxes(s0, -1, -2))
    return y, jnp.swapaxes(h_end, -1, -2)


RW_STEP_SEQS = 8


def _rw_step_kernel(r_ref, lw_ref, k_ref, v_ref, kk_ref, a_ref, s0_ref, seg_ref, y_ref, sT_ref, *, steps):
    nb = s0_ref.shape[0]
    rows = nb * RW_HD
    seg = seg_ref[...]
    st = s0_ref[...].reshape(rows, RW_W)
    vi = lax.broadcasted_iota(jnp.int32, (rows, RW_W), 0) % RW_HD
    ki = lax.broadcasted_iota(jnp.int32, (rows, RW_W), 1) % RW_HD
    dmask = vi == ki
    seg_sum = _seg_sum

    def bc(ref, t):
        return jnp.broadcast_to(ref[:, t:t + 1, :], (nb, RW_HD, RW_W)).reshape(rows, RW_W)

    for t in range(steps):
        kk = bc(kk_ref, t)
        sa = seg_sum(st * (-kk), seg)
        vcol = seg_sum(jnp.where(dmask, bc(v_ref, t), 0.0), seg)
        st = st * jnp.exp(bc(lw_ref, t)) + sa * (kk * bc(a_ref, t)) + vcol * bc(k_ref, t)
        ycol = seg_sum(st * bc(r_ref, t), seg)
        y_ref[:, t:t + 1, :] = jnp.sum(jnp.where(dmask, ycol, 0.0).reshape(nb, RW_HD, RW_W), axis=1, keepdims=True)
    sT_ref[...] = st.reshape(nb, RW_HD, RW_W)


def _rw_steps(r, lw, k, v, kk, a, s0, seg, b, t):
    nb = RW_STEP_SEQS
    row = pl.BlockSpec((nb, t, RW_W), lambda i: (i, 0, 0))
    st = pl.BlockSpec((nb, RW_HD, RW_W), lambda i: (i, 0, 0))
    slab = lambda s: s.transpose(0, 2, 1, 3).reshape(b, RW_HD, RW_W)
    y, s_end = pl.pallas_call(
        functools.partial(_rw_step_kernel, steps=t),
        grid=(b // nb,),
        in_specs=[row] * 6 + [st, pl.BlockSpec((RW_W, RW_W), lambda i: (0, 0))],
        out_specs=[row, st],
        out_shape=[jax.ShapeDtypeStruct((b, t, RW_W), F32), jax.ShapeDtypeStruct((b, RW_HD, RW_W), F32)],
        compiler_params=_cparams("parallel"),
        name="rwkv_steps",
    )(*(u.reshape(b, t, RW_W) for u in (r, lw, k, v, kk, a)), slab(s0), seg)
    return y.reshape(b * t, RW_W), s_end.reshape(b, RW_HD, RW_HEADS, RW_HD).transpose(0, 2, 1, 3)


def _rw_post_kernel(y_ref, r_ref, k_ref, v_ref, g_ref, rk_ref, lg_ref, lb_ref, seg_ref, o_ref):
    seg = seg_ref[...]
    y = y_ref[...]
    mean = _seg_sum(y, seg) * (1.0 / RW_HD)
    yc = y - mean
    var = _seg_sum(yc * yc, seg) * (1.0 / RW_HD)
    yn = yc * lax.rsqrt(var + RW_LN_EPS) * lg_ref[...] + lb_ref[...]
    v = v_ref[...]
    bonus = _seg_sum(r_ref[...] * k_ref[...] * rk_ref[...], seg) * v
    o_ref[...] = ((yn + bonus) * g_ref[...]).astype(BF16)


def _rw_post(y, r, k, v, g, rk, lg, lb, seg):
    m = y.shape[0]
    tm = min(512, m)
    row = pl.BlockSpec((tm, RW_W), lambda i: (i, 0))
    vec = pl.BlockSpec((1, RW_W), lambda i: (0, 0))
    return pl.pallas_call(
        _rw_post_kernel,
        grid=(m // tm,),
        in_specs=[row] * 5 + [vec] * 3 + [pl.BlockSpec((RW_W, RW_W), lambda i: (0, 0))],
        out_specs=row,
        out_shape=jax.ShapeDtypeStruct((m, RW_W), BF16),
        compiler_params=_cparams("parallel"),
        name="rwkv_post",
    )(y, r, k, v, g, rk, lg, lb, seg)


def _s5_mats(lam_re, lam_im, log_step, b_re, b_im, c_re, c_im, cs):
    delta = jnp.exp(log_step)[:, None]
    mag = jnp.exp(lam_re * delta)
    ab_re, ab_im = mag * jnp.cos(lam_im * delta), mag * jnp.sin(lam_im * delta)
    den = lam_re * lam_re + lam_im * lam_im
    n_re, n_im = ab_re - 1.0, ab_im
    q_re = (n_re * lam_re + n_im * lam_im) / den
    q_im = (n_im * lam_re - n_re * lam_im) / den
    bb_re = q_re[..., None] * b_re - q_im[..., None] * b_im
    bb_im = q_re[..., None] * b_im + q_im[..., None] * b_re
    tau = jnp.arange(cs + 1, dtype=F32)[:, None, None]
    pmag = jnp.exp(tau * (lam_re * delta)[None])
    pr, pi = pmag * jnp.cos(tau * (lam_im * delta)[None]), pmag * jnp.sin(tau * (lam_im * delta)[None])
    wr = pr[:cs, :, :, None] * bb_re[None] - pi[:cs, :, :, None] * bb_im[None]
    wi = pr[:cs, :, :, None] * bb_im[None] + pi[:cs, :, :, None] * bb_re[None]
    kern = (jnp.einsum('gcp,tgpd->tgdc', c_re, wr, precision=HI)
            - jnp.einsum('gcp,tgpd->tgdc', c_im, wi, precision=HI))
    lag = jnp.arange(cs)[None, :] - jnp.arange(cs)[:, None]
    toep = jnp.where((lag >= 0)[:, :, None, None, None], kern[jnp.maximum(lag, 0)], 0.0)
    toep = toep.transpose(2, 0, 3, 1, 4).reshape(S5_GROUPS, cs * S5_GC, cs * S5_GC)
    rev = jnp.arange(cs - 1, -1, -1)
    gmat = jnp.concatenate([wr[rev], wi[rev]], axis=2)
    gmat = gmat.transpose(1, 0, 3, 2).reshape(S5_GROUPS, cs * S5_GC, 2 * S5_P)
    p1r, p1i = pr[1:], pi[1:]
    h_re = c_re[None] * p1r[:, :, None, :] - c_im[None] * p1i[:, :, None, :]
    h_im = -c_re[None] * p1i[:, :, None, :] - c_im[None] * p1r[:, :, None, :]
    hmat = jnp.concatenate([h_re, h_im], axis=3)
    hmat = hmat.transpose(1, 3, 0, 2).reshape(S5_GROUPS, 2 * S5_P, cs * S5_GC)
    lam_a = jnp.concatenate([pr[cs], pr[cs]], axis=-1)[:, None, :]
    lam_b = jnp.concatenate([-pi[cs], pi[cs]], axis=-1)[:, None, :]
    return toep, gmat, hmat, lam_a, lam_b


def _s5_in_kernel(u_ref, gm_ref, o_ref):
    o_ref[0] = _dot3(_split(u_ref[0]), _split(gm_ref[0]), NN_DIMS)


def _s5_carry_kernel(g_ref, x0_ref, la_ref, lb_ref, xs_ref, xT_ref, *, nj):
    gt, bsz = x0_ref.shape[0], x0_ref.shape[1]
    la = jnp.broadcast_to(la_ref[...], (gt, bsz, LANES)).reshape(gt * bsz, LANES)
    lb = jnp.broadcast_to(lb_ref[...], (gt, bsz, LANES)).reshape(gt * bsz, LANES)

    def step(j, x):
        xs_ref[:, j] = x.reshape(gt, bsz, LANES)
        return x * la + pltpu.roll(x, S5_P, 1) * lb + g_ref[:, j].reshape(gt * bsz, LANES)

    x = lax.fori_loop(0, nj, step, x0_ref[...].reshape(gt * bsz, LANES))
    xT_ref[...] = x.reshape(gt, bsz, LANES)


def _s5_out_kernel(u_ref, xs_ref, tp_ref, hm_ref, d_ref, o_ref):
    u = u_ref[0]
    y = (_dot3(_split(u), _split(tp_ref[0]), NN_DIMS) + _dot3(_split(xs_ref[0]), _split(hm_ref[0]), NN_DIMS))
    o_ref[0] = y + d_ref[0] * u


def _s5_scan(ug, x0, mats, dg, nj, bsz):
    toep, gmat, hmat, lam_a, lam_b = mats
    ng, rows, w = ug.shape
    grp = lambda r, c: pl.BlockSpec((1, r, c), lambda g: (g, 0, 0))
    gin = pl.pallas_call(
        _s5_in_kernel,
        grid=(ng,),
        in_specs=[grp(rows, w), grp(w, LANES)],
        out_specs=grp(rows, LANES),
        out_shape=jax.ShapeDtypeStruct((ng, rows, LANES), F32),
        compiler_params=_cparams("parallel"),
        name="s5_chunk_in",
    )(ug, gmat)
    gt = 8
    xs, x_last = pl.pallas_call(
        functools.partial(_s5_carry_kernel, nj=nj),
        grid=(ng // gt,),
        in_specs=[pl.BlockSpec((gt, nj, bsz, LANES), lambda g: (g, 0, 0, 0)),
                  pl.BlockSpec((gt, bsz, LANES), lambda g: (g, 0, 0)),
                  pl.BlockSpec((gt, 1, LANES), lambda g: (g, 0, 0)),
                  pl.BlockSpec((gt, 1, LANES), lambda g: (g, 0, 0))],
        out_specs=[pl.BlockSpec((gt, nj, bsz, LANES), lambda g: (g, 0, 0, 0)),
                   pl.BlockSpec((gt, bsz, LANES), lambda g: (g, 0, 0))],
        out_shape=[jax.ShapeDtypeStruct((ng, nj, bsz, LANES), F32),
                   jax.ShapeDtypeStruct((ng, bsz, LANES), F32)],
        compiler_params=_cparams("parallel"),
        name="s5_carry",
    )(gin.reshape(ng, nj, bsz, LANES), x0, lam_a, lam_b)
    yg = pl.pallas_call(
        _s5_out_kernel,
        grid=(ng,),
        in_specs=[grp(rows, w), grp(rows, LANES), grp(w, w), grp(LANES, w), grp(1, w)],
        out_specs=grp(rows, w),
        out_shape=jax.ShapeDtypeStruct((ng, rows, w), F32),
        compiler_params=_cparams("parallel"),
        name="s5_chunk_out",
    )(ug, xs.reshape(ng, rows, LANES), toep, hmat, dg)
    return yg, x_last


def _s5_glu_kernel(y_ref, w1_ref, w2_ref, o_ref):
    y = y_ref[...]
    y = y * (0.5 * (1.0 + jnp.tanh(math.sqrt(2.0 / math.pi) * (y + 0.044715 * (y * y * y)))))
    yb = y.astype(BF16)
    o_ref[...] = (jnp.dot(yb, w1_ref[...], preferred_element_type=F32)
                  * _sigmoid(jnp.dot(yb, w2_ref[...], preferred_element_type=F32))).astype(BF16)


def _s5_glu(y, w1, w2):
    m = y.shape[0]
    tm = min(512, m)
    row = pl.BlockSpec((tm, S5_W), lambda i: (i, 0))
    mat = pl.BlockSpec((S5_W, S5_W), lambda i: (0, 0))
    return pl.pallas_call(
        _s5_glu_kernel,
        grid=(m // tm,),
        in_specs=[row, mat, mat],
        out_specs=row,
        out_shape=jax.ShapeDtypeStruct((m, S5_W), BF16),
        compiler_params=_cparams("parallel"),
        name="s5_glu",
    )(y, w1, w2)


def _s5(zs, x0_re, x0_im, mats, dskip, glu1, glu2, b, t, cs):
    nj = t // cs
    w = cs * S5_GC
    wp = max(w, LANES)
    lpad = lambda m, axes: jnp.pad(m, [(0, wp - w) if ax in axes else (0, 0) for ax in range(m.ndim)])
    toep, gmat, hmat, lam_a, lam_b = mats
    mats = (lpad(toep, (1, 2)), lpad(gmat, (1,)), lpad(hmat, (2,)), lam_a, lam_b)
    ug = zs.reshape(b, nj, cs, S5_GROUPS, S5_GC).transpose(3, 1, 0, 2, 4).reshape(S5_GROUPS, nj * b, w)
    x0 = jnp.concatenate([x0_re, x0_im], axis=-1).transpose(1, 0, 2)
    dg = jnp.tile(dskip.reshape(S5_GROUPS, 1, S5_GC), (1, 1, cs))
    yg, x_last = _s5_scan(lpad(ug, (2,)), x0, mats, lpad(dg, (2,)), nj, b)
    y = yg[:, :, :w].reshape(S5_GROUPS, nj, b, cs, S5_GC).transpose(2, 1, 3, 0, 4).reshape(b * t, S5_W)
    x_last = x_last.transpose(1, 0, 2)
    return _s5_glu(y, glu1, glu2), x_last[..., :S5_P], x_last[..., S5_P:]


def _mix_out_kernel(x_ref, ya_ref, yr_ref, ys_ref, w_ref, o_ref):
    acc = jnp.dot(ya_ref[...].astype(BF16), w_ref[:ATT_W, :], preferred_element_type=F32)
    acc = acc + jnp.dot(yr_ref[...], w_ref[ATT_W:ATT_W + RW_W, :], preferred_element_type=F32)
    acc = acc + jnp.dot(ys_ref[...], w_ref[ATT_W + RW_W:, :], preferred_element_type=F32)
    o_ref[...] = x_ref[...] + acc


def _mix_out(x, ya, yr, ys, w):
    m, d = x.shape
    tm = min(512, m)
    row = lambda c: pl.BlockSpec((tm, c), lambda i: (i, 0))
    return pl.pallas_call(
        _mix_out_kernel,
        grid=(m // tm,),
        in_specs=[row(d), row(ATT_W), row(RW_W), row(S5_W), pl.BlockSpec(w.shape, lambda i: (0, 0))],
        out_specs=row(d),
        out_shape=jax.ShapeDtypeStruct((m, d), F32),
        compiler_params=_cparams("parallel"),
        name="mix_out",
    )(x, ya, yr, ys, w)


def _layer_weights(p, l):
    ffpad = lambda w: jnp.pad(w, ((0, 0), (0, D_FF_PAD - D_FF))).astype(BF16)
    w_in = p['w_in'][l]
    w = {}
    for n in ('ffn1', 'ffn2'):
        w[n] = (p[n + '_norm'][l][None], ffpad(p[n + '_wg'][l]), ffpad(p[n + '_wu'][l]),
                jnp.pad(p[n + '_wd'][l], ((0, D_FF_PAD - D_FF), (0, 0))).astype(BF16))
    w['mix_norm'] = p['mix_norm'][l][None]
    w['w_att'] = jnp.pad(w_in[:, :ATT_IN].astype(BF16), ((0, 0), (0, ATT_Z - ATT_IN)))
    w['w_rw'] = w_in[:, ATT_IN:ATT_IN + RW_IN].astype(BF16)
    w['w_s5'] = w_in[:, ATT_IN + RW_IN:].astype(BF16)
    w['q_norm'] = p['q_norm'][l][None]
    w['k_norm'] = p['k_norm'][l][None]
    tailpad = lambda m, off: jnp.pad(m, ((off, LANES - off - m.shape[0]), (0, 0)))
    w['rw_prep'] = (p['rw_mu'][l][None], p['rw_w0'][l][None], p['rw_a0'][l][None], p['rw_kk'][l][None],
                    p['rw_ka'][l][None], tailpad(p['rw_w2'][l], 0), tailpad(p['rw_a2'][l], RW_DECAY_R),
                    tailpad(p['rw_g2'][l], RW_DECAY_R + RW_A_R), _head_ones())
    w['rw_post'] = (p['rw_rk'][l].reshape(1, RW_W), p['rw_lnx_g'][l][None], p['rw_lnx_b'][l][None], _head_ones())
    w['s5_params'] = (p['s5_lam_re'][l], p['s5_lam_im'][l], p['s5_log_step'][l], p['s5_b_re'][l], p['s5_b_im'][l],
                      p['s5_c_re'][l], p['s5_c_im'][l])
    w['s5_d'] = p['s5_d'][l]
    w['s5_glu'] = (p['s5_glu1'][l].astype(BF16), p['s5_glu2'][l].astype(BF16))
    w['w_out'] = p['w_out'][l].astype(BF16)
    return w


def _layer(x, w, tabs, b, t, attn_fn, shift0, rw0, s5re0, s5im0, s5_chunk):
    m = b * t
    x = _ffn(x, *w['ffn1'])
    za = _nmm(x, w['mix_norm'], w['w_att'], ATT_Z // 3)
    zr = _nmm(x, w['mix_norm'], w['w_rw'], RW_IN)
    zs = _nmm(x, w['mix_norm'], w['w_s5'], S5_W)
    qb, k32, kb, v32, vb, qib, ki32, kid, wi = _attn_prep(za, tabs, w['q_norm'], w['k_norm'])
    ya = attn_fn(qb, kb, vb, v32, qib, kid, wi)
    zr3 = zr.reshape(b, t, RW_IN)
    prev = jnp.concatenate([shift0[:, None, :], zr3[:, :-1]], axis=1).reshape(m, RW_IN)
    r, lw, k2, v, kk, a, g = _rw_prep(zr, prev, *w['rw_prep'])
    if t % RW_CHUNK == 0:
        y, rw_new = _rw_scan(r, lw, k2, v, kk, a, rw0, b, t)
    else:
        y, rw_new = _rw_steps(r, lw, k2, v, kk, a, rw0, w['rw_post'][3], b, t)
    yr = _rw_post(y, r, k2, v, g, *w['rw_post'])
    mats = _s5_mats(*w['s5_params'], s5_chunk)
    ys, s5re, s5im = _s5(zs, s5re0, s5im0, mats, w['s5_d'], *w['s5_glu'], b, t, s5_chunk)
    x = _mix_out(x, ya, yr, ys, w['w_out'])
    x = _ffn(x, *w['ffn2'])
    state = (k32.reshape(b, t, ATT_HEADS, ATT_HD), v32.reshape(b, t, ATT_HEADS, ATT_HD),
             ki32.reshape(b, t, IDX_HD), rw_new, zr3[:, -1], s5re, s5im)
    return x, state


def kernel(x_prompt, x_sample, cache_k, cache_v, cache_kidx, state_rwkv, state_shift, state_s5_re, state_s5_im, page_table, ffn1_norm, ffn1_wg, ffn1_wu, ffn1_wd, mix_norm, w_in, q_norm, k_norm, rw_mu, rw_w0, rw_w2, rw_a0, rw_a2, rw_g2, rw_kk, rw_ka, rw_rk, rw_lnx_g, rw_lnx_b, s5_lam_re, s5_lam_im, s5_log_step, s5_b_re, s5_b_im, s5_c_re, s5_c_im, s5_d, s5_glu1, s5_glu2, w_out, ffn2_norm, ffn2_wg, ffn2_wu, ffn2_wd):
    p = dict(ffn1_norm=ffn1_norm, ffn1_wg=ffn1_wg, ffn1_wu=ffn1_wu, ffn1_wd=ffn1_wd, mix_norm=mix_norm, w_in=w_in,
             q_norm=q_norm, k_norm=k_norm, rw_mu=rw_mu, rw_w0=rw_w0, rw_w2=rw_w2, rw_a0=rw_a0, rw_a2=rw_a2,
             rw_g2=rw_g2, rw_kk=rw_kk, rw_ka=rw_ka, rw_rk=rw_rk, rw_lnx_g=rw_lnx_g, rw_lnx_b=rw_lnx_b,
             s5_lam_re=s5_lam_re, s5_lam_im=s5_lam_im, s5_log_step=s5_log_step, s5_b_re=s5_b_re, s5_b_im=s5_b_im,
             s5_c_re=s5_c_re, s5_c_im=s5_c_im, s5_d=s5_d, s5_glu1=s5_glu1, s5_glu2=s5_glu2, w_out=w_out,
             ffn2_norm=ffn2_norm, ffn2_wg=ffn2_wg, ffn2_wu=ffn2_wu, ffn2_wd=ffn2_wd)
    depth = w_in.shape[0]
    b, t, d = x_prompt.shape
    db, s, _ = x_sample.shape
    past = page_table.shape[1] * PAGE_SIZE
    weights = [_layer_weights(p, l) for l in range(depth)]

    tabs_p = _rope_tables(jnp.arange(t))
    attn_p = lambda qb, kb, vb, v32, qib, kid, wi: _dsa_prompt(qb, kb, vb, qib, kid, wi, b, t)
    y = x_prompt.reshape(b * t, d)
    st_p = []
    for l in range(depth):
        y, st = _layer(y, weights[l], tabs_p, b, t, attn_p, jnp.zeros((b, RW_IN), F32),
                       jnp.zeros((b, RW_HEADS, RW_HD, RW_HD), F32), jnp.zeros((b, S5_GROUPS, S5_P), F32),
                       jnp.zeros((b, S5_GROUPS, S5_P), F32), s5_chunk=16)
        st_p.append(st)
    y_prompt = y.reshape(b, t, d)

    ms = db * s
    tm_s = min(512, ms)
    tabs_s = tuple(jnp.tile(tb, (tm_s // s, 1)) for tb in _rope_tables(past + jnp.arange(s)))
    y = x_sample.reshape(ms, d)
    st_s = []
    for l in range(depth):
        attn_s = functools.partial(
            lambda qb, kb, vb, v32, qib, kid, wi, l: _dsa_sample(qb, kb, v32, qib, kid, wi, cache_k, cache_v,
                                                                  cache_kidx, l, page_table, db, s), l=l)
        y, st = _layer(y, weights[l], tabs_s, db, s, attn_s, state_shift[l], state_rwkv[l], state_s5_re[l],
                       state_s5_im[l], s5_chunk=s)
        st_s.append(st)
    y_sample = y.reshape(db, s, d)
    outs_p = [jnp.stack(u) for u in zip(*st_p)]
    outs_s = [jnp.stack(u) for u in zip(*st_s)]
    return (y_prompt, y_sample, *outs_p, *outs_s)
```
